```python
import math
import jax, jax.numpy as jnp
from jax import lax
import numpy as np

D_MODEL = 1024
BATCH = 2
SEQ = 8192
DEPTH = 4

N_MIXERS = 3
N_A = (DEPTH + 2) // 3
N_B = (DEPTH + 1) // 3
N_C = DEPTH // 3

CONV_WIDTH = 3

RET_HEADS = 4
RET_DK = D_MODEL // RET_HEADS
RET_DV = 2 * RET_DK
RET_VDIM = RET_HEADS * RET_DV
RET_CHUNK = 128
ROPE_BASE = 10000.0

SGU_HALF = 3 * D_MODEL
SGU_GROUPS = 4
SGU_GD = SGU_HALF // SGU_GROUPS
SGU_CHUNK = 128

N_EXPERTS = 16
N_GROUPS = 4
EXPERTS_PER_GROUP = N_EXPERTS // N_GROUPS
TOP_K = 2
D_EXPERT = D_MODEL // 2

PLE_DIM = 256
ALPHA = (2 * DEPTH) ** 0.25
BETA = (8 * DEPTH) ** -0.25
LN_EPS = 1e-5

kernel_name = "hybrid_conv_retention_sgu_moe_deepnorm"


def _layer_norm(x, g, b):
    xf = x.astype(jnp.float32)
    mu = jnp.mean(xf, axis=-1, keepdims=True)
    xc = xf - mu
    var = jnp.mean(xc * xc, axis=-1, keepdims=True)
    return (xc * lax.rsqrt(var + LN_EPS) * g + b).astype(x.dtype)


def _short_conv_mixer(x, w_in, conv_w, w_out):
    bch = x @ w_in
    b_gate, c_gate, h = jnp.split(bch, 3, axis=-1)
    z = c_gate * h
    zc = lax.conv_general_dilated(
        z, conv_w[:, None, :].astype(z.dtype),
        window_strides=(1,), padding=[(CONV_WIDTH - 1, 0)],
        dimension_numbers=("NWC", "WIO", "NWC"),
        feature_group_count=D_MODEL)
    return (b_gate * zc) @ w_out


def _rotary(t, cos, sin):
    t1, t2 = jnp.split(t, 2, axis=-1)
    return jnp.concatenate([t1 * cos - t2 * sin, t2 * cos + t1 * sin], axis=-1)


def _retention_mixer(x, w_in, w_out):
    bsz, seq, _ = x.shape
    n_chunks = seq // RET_CHUNK
    proj = x @ w_in
    q, k, v, g = jnp.split(proj, [D_MODEL, 2 * D_MODEL, 2 * D_MODEL + RET_VDIM], axis=-1)
    q = q.astype(jnp.float32).reshape(bsz, seq, RET_HEADS, RET_DK)
    k = k.astype(jnp.float32).reshape(bsz, seq, RET_HEADS, RET_DK) * (RET_DK ** -0.5)
    v = v.astype(jnp.float32).reshape(bsz, seq, RET_HEADS, RET_DV)

    pos = jnp.arange(seq, dtype=jnp.float32)
    inv_freq = 1.0 / (ROPE_BASE ** jnp.linspace(0.0, 1.0, RET_DK // 2, dtype=jnp.float32))
    ang = pos[:, None] * inv_freq[None, :]
    cos = jnp.cos(ang)[:, None, :]
    sin = jnp.sin(ang)[:, None, :]
    q = _rotary(q, cos, sin)
    k = _rotary(k, cos, sin)

    def to_chunks(t):
        return t.reshape(bsz, n_chunks, RET_CHUNK, RET_HEADS, -1).transpose(1, 0, 3, 2, 4)
    qc, kc, vc = to_chunks(q), to_chunks(k), to_chunks(v)

    log_gamma = jnp.log(1.0 - 2.0 ** (-5.0 - jnp.arange(RET_HEADS, dtype=jnp.float32)))
    idx = jnp.arange(RET_CHUNK, dtype=jnp.float32)
    diff = idx[:, None] - idx[None, :]
    decay_mask = jnp.where(diff >= 0,
                           jnp.exp(log_gamma[:, None, None] * jnp.maximum(diff, 0.0)), 0.0)
    q_decay = jnp.exp(log_gamma[:, None] * (idx[None, :] + 1.0))
    k_decay = jnp.exp(log_gamma[:, None] * (RET_CHUNK - 1.0 - idx[None, :]))
    chunk_decay = jnp.exp(log_gamma * RET_CHUNK)

    scores = jnp.einsum("nbhcd,nbhsd->nbhcs", qc, kc) * decay_mask[None, None]
    inner = jnp.einsum("nbhcs,nbhse->nbhce", scores, vc)

    def step(state, inp):
        q_i, k_i, v_i = inp
        cross = jnp.einsum("bhcd,bhde->bhce", q_i, state) * q_decay[None, :, :, None]
        state = state * chunk_decay[None, :, None, None] + jnp.einsum(
            "bhcd,bhce->bhde", k_i * k_decay[None, :, :, None], v_i)
        return state, cross
    state0 = jnp.zeros((bsz, RET_HEADS, RET_DK, RET_DV), jnp.float32)
    _, cross = lax.scan(step, state0, (qc, kc, vc))

    o = (inner + cross).transpose(1, 0, 3, 2, 4).reshape(bsz, seq, RET_HEADS, RET_DV)
    mu = jnp.mean(o, axis=-1, keepdims=True)
    oc = o - mu
    o = oc * lax.rsqrt(jnp.mean(oc * oc, axis=-1, keepdims=True) + LN_EPS)
    o = o.reshape(bsz, seq, RET_VDIM).astype(x.dtype)
    return (jax.nn.silu(g) * o) @ w_out


def _sgu_mixer(x, w_in, ln_g, ln_b, w_s, b_s, w_out):
    bsz, seq, _ = x.shape
    n_chunks = seq // SGU_CHUNK
    z = jax.nn.gelu(x @ w_in, approximate=False)
    u, v = jnp.split(z, 2, axis=-1)
    v = _layer_norm(v, ln_g, ln_b).reshape(bsz, n_chunks, SGU_CHUNK, SGU_GROUPS, SGU_GD)
    causal = jnp.tril(jnp.ones((SGU_CHUNK, SGU_CHUNK), w_s.dtype))
    ws = w_s * causal[None]
    vs = jnp.einsum("gts,bnsgd->bntgd", ws, v) + b_s.T[None, None, :, :, None]
    return (u * vs.reshape(bsz, seq, SGU_HALF)) @ w_out


def _moe(h, router_w, router_b, w_gate, w_up, w_down):
    bsz, seq, d = h.shape
    hf = h.reshape(bsz * seq, d)
    scores = jax.nn.sigmoid((hf @ router_w).astype(jnp.float32))
    sel = scores + router_b.astype(jnp.float32)
    grp = lax.top_k(sel.reshape(-1, N_GROUPS, EXPERTS_PER_GROUP), 2)[0].sum(-1)
    g_idx = jnp.argmax(grp, axis=-1)
    in_group = (jnp.arange(N_EXPERTS) // EXPERTS_PER_GROUP)[None, :] == g_idx[:, None]
    _, e_idx = lax.top_k(jnp.where(in_group, sel, -jnp.inf), TOP_K)
    w = jnp.take_along_axis(scores, e_idx, axis=-1)
    w = w / jnp.sum(w, axis=-1, keepdims=True)
    combine = jnp.sum(jax.nn.one_hot(e_idx, N_EXPERTS, dtype=jnp.float32) * w[..., None],
                      axis=1).astype(h.dtype)
    y = jnp.zeros_like(hf)
    for e in range(N_EXPERTS):
        he = jax.nn.silu(hf @ w_gate[e]) * (hf @ w_up[e])
        y = y + combine[:, e:e + 1] * (he @ w_down[e])
    return y.reshape(bsz, seq, d)


def setup_inputs(seed: int = 0) -> dict:
    key = jax.random.key(seed)
    ks = jax.random.split(key, 24)

    def nrm(k, shape, scale):
        return jax.random.normal(k, shape, jnp.float32) * scale

    return {
        "x": nrm(ks[0], (BATCH, SEQ, D_MODEL), 1.0),
        "p": nrm(ks[1], (DEPTH, BATCH, SEQ, PLE_DIM), 1.0),
        "a_w_in": nrm(ks[2], (N_A, D_MODEL, 3 * D_MODEL), D_MODEL ** -0.5),
        "a_conv_w": nrm(ks[3], (N_A, CONV_WIDTH, D_MODEL), CONV_WIDTH ** -0.5),
        "a_w_out": nrm(ks[4], (N_A, D_MODEL, D_MODEL), D_MODEL ** -0.5 * BETA),
        "b_w_in": nrm(ks[5], (N_B, D_MODEL, 2 * D_MODEL + 2 * RET_VDIM), D_MODEL ** -0.5),
        "b_w_out": nrm(ks[6], (N_B, RET_VDIM, D_MODEL), RET_VDIM ** -0.5 * BETA),
        "c_w_in": nrm(ks[7], (N_C, D_MODEL, 2 * SGU_HALF), D_MODEL ** -0.5),
        "c_ln_g": 1.0 + nrm(ks[8], (N_C, SGU_HALF), 0.02),
        "c_ln_b": nrm(ks[9], (N_C, SGU_HALF), 0.02),
        "c_w_s": nrm(ks[10], (N_C, SGU_GROUPS, SGU_CHUNK, SGU_CHUNK), 0.5 * SGU_CHUNK ** -0.5),
        "c_b_s": 1.0 + nrm(ks[11], (N_C, SGU_GROUPS, SGU_CHUNK), 0.1),
        "c_w_out": nrm(ks[12], (N_C, SGU_HALF, D_MODEL), SGU_HALF ** -0.5 * BETA),
        "router_w": nrm(ks[13], (D_MODEL, N_EXPERTS), D_MODEL ** -0.5),
        "router_b": nrm(ks[14], (N_EXPERTS,), 0.01),
        "moe_w_gate": nrm(ks[15], (DEPTH, N_EXPERTS, D_MODEL, D_EXPERT), D_MODEL ** -0.5),
        "moe_w_up": nrm(ks[16], (DEPTH, N_EXPERTS, D_MODEL, D_EXPERT), D_MODEL ** -0.5),
        "moe_w_down": nrm(ks[17], (DEPTH, N_EXPERTS, D_EXPERT, D_MODEL), D_EXPERT ** -0.5 * BETA),
        "ln_g": 1.0 + nrm(ks[18], (DEPTH, 2, D_MODEL), 0.02),
        "ln_b": nrm(ks[19], (DEPTH, 2, D_MODEL), 0.02),
        "ple_w_proj": nrm(ks[20], (DEPTH, PLE_DIM, D_MODEL), PLE_DIM ** -0.5),
        "ple_w_gate": nrm(ks[21], (DEPTH, D_MODEL, D_MODEL), D_MODEL ** -0.5),
        "ple_b_gate": nrm(ks[22], (DEPTH, D_MODEL), 0.02),
    }


def reference(x, p, a_w_in, a_conv_w, a_w_out, b_w_in, b_w_out, c_w_in, c_ln_g, c_ln_b,
              c_w_s, c_b_s, c_w_out, router_w, router_b, moe_w_gate, moe_w_up, moe_w_down,
              ln_g, ln_b, ple_w_proj, ple_w_gate, ple_b_gate):
    for i in range(DEPTH):
        mixer, slot = i % N_MIXERS, i // N_MIXERS
        if mixer == 0:
            mix = _short_conv_mixer(x, a_w_in[slot], a_conv_w[slot], a_w_out[slot])
        elif mixer == 1:
            mix = _retention_mixer(x, b_w_in[slot], b_w_out[slot])
        else:
            mix = _sgu_mixer(x, c_w_in[slot], c_ln_g[slot], c_ln_b[slot],
                             c_w_s[slot], c_b_s[slot], c_w_out[slot])
        x = _layer_norm(ALPHA * x + mix, ln_g[i, 0], ln_b[i, 0])
        ffn = _moe(x, router_w, router_b, moe_w_gate[i], moe_w_up[i], moe_w_down[i])
        x = _layer_norm(ALPHA * x + ffn, ln_g[i, 1], ln_b[i, 1])
        emb = p[i] @ ple_w_proj[i]
        gate = jax.nn.sigmoid(x @ ple_w_gate[i] + ple_b_gate[i])
        x = x + gate * emb
    return x
```

```python
import functools
import math

import numpy as np
import jax
import jax.numpy as jnp
from jax import lax
from jax.experimental import pallas as pl
from jax.experimental.pallas import tpu as pltpu

F32 = jnp.float32
BF16 = jnp.bfloat16

D_MODEL = 1024
DEPTH = 4
N_MIXERS = 3
CONV_WIDTH = 3
RET_HEADS = 4
RET_DK = D_MODEL // RET_HEADS
RET_DV = 2 * RET_DK
RET_VDIM = RET_HEADS * RET_DV
RET_CHUNK = 128
ROPE_BASE = 10000.0
SGU_HALF = 3 * D_MODEL
SGU_GROUPS = 4
SGU_GD = SGU_HALF // SGU_GROUPS
SGU_CHUNK = 128
N_EXPERTS = 16
N_GROUPS = 4
EXPERTS_PER_GROUP = N_EXPERTS // N_GROUPS
D_EXPERT = D_MODEL // 2
PLE_DIM = 256
ALPHA = (2 * DEPTH) ** 0.25
LN_EPS = 1e-5

VMEM_LIMIT_BYTES = 56 * 1024 * 1024
SUBLANES = 8

TM_CONV = 512
TM_POST = 512
TM_MOE = 1024


def _const_spec(shape):
    nd = len(shape)
    return pl.BlockSpec(shape, lambda *_: (0,) * nd, pipeline_mode=pl.Buffered(1))


def _layer_norm(h, g, b):
    mu = jnp.mean(h, axis=-1, keepdims=True)
    hc = h - mu
    var = jnp.mean(hc * hc, axis=-1, keepdims=True)
    return hc * lax.rsqrt(var + LN_EPS) * g + b


def _top2_sum(a, b, c, d):
    m1, n1 = jnp.maximum(a, b), jnp.minimum(a, b)
    m2, n2 = jnp.maximum(c, d), jnp.minimum(c, d)
    return jnp.maximum(m1, m2) + jnp.maximum(jnp.minimum(m1, m2), jnp.maximum(n1, n2))


def _route(x1b, rwt_ref, rb_ref):
    tm = x1b.shape[0]
    logits_t = lax.dot_general(rwt_ref[...], x1b, (((1,), (1,)), ((), ())),
                               preferred_element_type=F32)
    scores = jax.nn.sigmoid(logits_t)
    sel = scores + rb_ref[...]
    s = [sel[e:e + 1, :] for e in range(N_EXPERTS)]
    sc = [scores[e:e + 1, :] for e in range(N_EXPERTS)]
    grp = [_top2_sum(*s[4 * g:4 * g + 4]) for g in range(N_GROUPS)]
    best, g_idx = grp[0], jnp.zeros((1, tm), jnp.int32)
    for g in range(1, N_GROUPS):
        better = grp[g] > best
        best = jnp.where(better, grp[g], best)
        g_idx = jnp.where(better, g, g_idx)

    def pick(rows, j):
        out = rows[j]
        for g in range(1, N_GROUPS):
            out = jnp.where(g_idx == g, rows[4 * g + j], out)
        return out
    v = [pick(s, j) for j in range(EXPERTS_PER_GROUP)]
    w = [pick(sc, j) for j in range(EXPERTS_PER_GROUP)]

    def argmax4(vals):
        bv, bi = vals[0], jnp.zeros((1, tm), jnp.int32)
        for j in range(1, EXPERTS_PER_GROUP):
            better = vals[j] > bv
            bv = jnp.where(better, vals[j], bv)
            bi = jnp.where(better, j, bi)
        return bi
    i1 = argmax4(v)
    i2 = argmax4([jnp.where(i1 == j, -jnp.inf, v[j]) for j in range(EXPERTS_PER_GROUP)])

    def take(vals, idx):
        out = vals[0]
        for j in range(1, EXPERTS_PER_GROUP):
            out = jnp.where(idx == j, vals[j], out)
        return out
    w1, w2 = take(w, i1), take(w, i2)
    den = w1 + w2
    w1, w2 = w1 / den, w2 / den
    e1, e2 = g_idx * EXPERTS_PER_GROUP + i1, g_idx * EXPERTS_PER_GROUP + i2
    eid = lax.broadcasted_iota(jnp.int32, (N_EXPERTS, tm), 0)
    return jnp.where(eid == e1, w1, 0.0) + jnp.where(eid == e2, w2, 0.0)


def _finish_mixer(x, mix, lng_ref, lnb_ref, rwt_ref, rb_ref, x1_ref, x1b_ref, comb_ref):
    tm = x.shape[0]
    x1 = _layer_norm(ALPHA * x + mix, lng_ref[...], lnb_ref[...])
    x1_ref[...] = x1
    x1b = x1.astype(BF16)
    x1b_ref[...] = x1b
    comb_t = _route(x1b, rwt_ref, rb_ref)
    pad = jnp.zeros((128 - N_EXPERTS, tm), F32)
    comb_ref[...] = jnp.concatenate([comb_t, pad], axis=0).T


def _conv_mixer_kernel(tiles_per_seq, x_ref, win_ref, cw_ref, wout_ref, lng_ref, lnb_ref,
                       rwt_ref, rb_ref, x1_ref, x1b_ref, comb_ref, carry_ref):
    i = pl.program_id(0)
    tm = x_ref.shape[0]
    x = x_ref[...]
    bch = jnp.dot(x.astype(BF16), win_ref[...], preferred_element_type=F32)
    b_gate = bch[:, :D_MODEL]
    z = bch[:, D_MODEL:2 * D_MODEL] * bch[:, 2 * D_MODEL:]

    @pl.when(i % tiles_per_seq == 0)
    def _():
        carry_ref[...] = jnp.zeros_like(carry_ref)
    prev = carry_ref[...]
    row = lax.broadcasted_iota(jnp.int32, (tm, 1), 0)
    p1, p2 = prev[SUBLANES - 1:SUBLANES], prev[SUBLANES - 2:SUBLANES - 1]
    z1 = jnp.where(row == 0, p1, pltpu.roll(z, 1, axis=0))
    z2 = jnp.where(row == 0, p2, jnp.where(row == 1, p1, pltpu.roll(z, 2, axis=0)))
    carry_ref[...] = z[tm - SUBLANES:, :]
    cw = cw_ref[...]
    zc = cw[0:1] * z2 + cw[1:2] * z1 + cw[2:3] * z
    mix = jnp.dot((b_gate * zc).astype(BF16), wout_ref[...], preferred_element_type=F32)
    _finish_mixer(x, mix, lng_ref, lnb_ref, rwt_ref, rb_ref, x1_ref, x1b_ref, comb_ref)


def _mixer_out_shapes(t):
    return (jax.ShapeDtypeStruct((t, D_MODEL), F32),
            jax.ShapeDtypeStruct((t, D_MODEL), BF16),
            jax.ShapeDtypeStruct((t, 128), F32))


def _mixer_out_specs(tm):
    return (pl.BlockSpec((tm, D_MODEL), lambda i: (i, 0)),
            pl.BlockSpec((tm, D_MODEL), lambda i: (i, 0)),
            pl.BlockSpec((tm, 128), lambda i: (i, 0)))


def _conv_mixer(x, seq, w_in, conv_w, w_out, ln_g, ln_b, rwt, rb):
    t = x.shape[0]
    tm = TM_CONV
    return pl.pallas_call(
        functools.partial(_conv_mixer_kernel, seq // tm),
        out_shape=_mixer_out_shapes(t),
        grid=(t // tm,),
        in_specs=[pl.BlockSpec((tm, D_MODEL), lambda i: (i, 0)),
                  _const_spec(w_in.shape), _const_spec(conv_w.shape), _const_spec(w_out.shape),
                  _const_spec(ln_g.shape), _const_spec(ln_b.shape),
                  _const_spec(rwt.shape), _const_spec(rb.shape)],
        out_specs=_mixer_out_specs(tm),
        scratch_shapes=[pltpu.VMEM((SUBLANES, D_MODEL), F32)],
        compiler_params=pltpu.CompilerParams(dimension_semantics=("arbitrary",),
                                             vmem_limit_bytes=VMEM_LIMIT_BYTES),
        name="conv_mixer",
    )(x, w_in, conv_w, w_out, ln_g, ln_b, rwt, rb)


def _rotary(t, cos, sin):
    half = t.shape[-1] // 2
    t1, t2 = t[:, :half], t[:, half:]
    return jnp.concatenate([t1 * cos - t2 * sin, t2 * cos + t1 * sin], axis=-1)


def _retention_kernel(chunks_per_seq, x_ref, win_ref, wout_ref, cos_ref, sin_ref, dmask_ref,
                      qdec_ref, kdec_ref, cdec_ref, lng_ref, lnb_ref, rwt_ref, rb_ref,
                      x1_ref, x1b_ref, comb_ref, state_ref):
    i = pl.program_id(0)

    @pl.when(i % chunks_per_seq == 0)
    def _():
        state_ref[...] = jnp.zeros_like(state_ref)

    x = x_ref[...]
    proj = jnp.dot(x.astype(BF16), win_ref[...], preferred_element_type=F32)
    cos, sin = cos_ref[...], sin_ref[...]
    gated = []
    for h in range(RET_HEADS):
        q = _rotary(proj[:, h * RET_DK:(h + 1) * RET_DK], cos, sin)
        k = _rotary(proj[:, D_MODEL + h * RET_DK:D_MODEL + (h + 1) * RET_DK] * (RET_DK ** -0.5),
                    cos, sin)
        v = proj[:, 2 * D_MODEL + h * RET_DV:2 * D_MODEL + (h + 1) * RET_DV]
        g = proj[:, 2 * D_MODEL + RET_VDIM + h * RET_DV:2 * D_MODEL + RET_VDIM + (h + 1) * RET_DV]
        qb, vb = q.astype(BF16), v.astype(BF16)
        scores = lax.dot_general(qb, k.astype(BF16), (((1,), (1,)), ((), ())),
                                 preferred_element_type=F32) * dmask_ref[h]
        inner = jnp.dot(scores.astype(BF16), vb, preferred_element_type=F32)
        state = state_ref[h]
        cross = jnp.dot(qb, state.astype(BF16), preferred_element_type=F32) * qdec_ref[h]
        kd = (k * kdec_ref[h]).astype(BF16)
        state_ref[h] = state * cdec_ref[h] + lax.dot_general(
            kd, vb, (((0,), (0,)), ((), ())), preferred_element_type=F32)
        o = inner + cross
        mu = jnp.mean(o, axis=-1, keepdims=True)
        oc = o - mu
        o = oc * lax.rsqrt(jnp.mean(oc * oc, axis=-1, keepdims=True) + LN_EPS)
        gated.append((jax.nn.silu(g) * o).astype(BF16))
    mix = jnp.dot(jnp.concatenate(gated, axis=-1), wout_ref[...], preferred_element_type=F32)
    _finish_mixer(x, mix, lng_ref, lnb_ref, rwt_ref, rb_ref, x1_ref, x1b_ref, comb_ref)


def _retention_tables(seq):
    c = RET_CHUNK
    pos = jnp.arange(seq, dtype=F32)
    inv_freq = 1.0 / (ROPE_BASE ** jnp.linspace(0.0, 1.0, RET_DK // 2, dtype=F32))
    ang = pos[:, None] * inv_freq[None, :]
    log_gamma = jnp.log(1.0 - 2.0 ** (-5.0 - jnp.arange(RET_HEADS, dtype=F32)))
    idx = jnp.arange(c, dtype=F32)
    diff = idx[:, None] - idx[None, :]
    dmask = jnp.where(diff >= 0, jnp.exp(log_gamma[:, None, None] * jnp.maximum(diff, 0.0)), 0.0)
    qdec = jnp.exp(log_gamma[:, None] * (idx[None, :] + 1.0))[:, :, None]
    kdec = jnp.exp(log_gamma[:, None] * (c - 1.0 - idx[None, :]))[:, :, None]
    cdec = jnp.broadcast_to(jnp.exp(log_gamma * c)[:, None, None], (RET_HEADS, 1, RET_DV))
    return jnp.cos(ang), jnp.sin(ang), dmask, qdec, kdec, cdec


def _retention_mixer(x, seq, w_in, w_out, ln_g, ln_b, rwt, rb):
    t = x.shape[0]
    c = RET_CHUNK
    cps = seq // c
    cos, sin, dmask, qdec, kdec, cdec = _retention_tables(seq)
    return pl.pallas_call(
        functools.partial(_retention_kernel, cps),
        out_shape=_mixer_out_shapes(t),
        grid=(t // c,),
        in_specs=[pl.BlockSpec((c, D_MODEL), lambda i: (i, 0)),
                  _const_spec(w_in.shape), _const_spec(w_out.shape),
                  pl.BlockSpec((c, RET_DK // 2), lambda i: (i % cps, 0)),
                  pl.BlockSpec((c, RET_DK // 2), lambda i: (i % cps, 0)),
                  _const_spec(dmask.shape), _const_spec(qdec.shape), _const_spec(kdec.shape),
                  _const_spec(cdec.shape),
                  _const_spec(ln_g.shape), _const_spec(ln_b.shape),
                  _const_spec(rwt.shape), _const_spec(rb.shape)],
        out_specs=_mixer_out_specs(c),
        scratch_shapes=[pltpu.VMEM((RET_HEADS, RET_DK, RET_DV), F32)],
        compiler_params=pltpu.CompilerParams(dimension_semantics=("arbitrary",),
                                             vmem_limit_bytes=VMEM_LIMIT_BYTES),
        name="retention_mixer",
    )(x, w_in, w_out, cos, sin, dmask, qdec, kdec, cdec, ln_g, ln_b, rwt, rb)


def _sgu_kernel(x_ref, win_ref, sg_ref, sb_ref, ws_ref, bs_ref, wout_ref, lng_ref, lnb_ref,
                rwt_ref, rb_ref, x1_ref, x1b_ref, comb_ref):
    c = x_ref.shape[0]
    x = x_ref[...]
    zin = jnp.dot(x.astype(BF16), win_ref[...], preferred_element_type=F32)
    z = 0.5 * zin * (1.0 + lax.erf(zin * (2.0 ** -0.5)))
    u = z[:, :SGU_HALF]
    v = _layer_norm(z[:, SGU_HALF:], sg_ref[...], sb_ref[...]).astype(BF16)
    row = lax.broadcasted_iota(jnp.int32, (c, c), 0)
    col = lax.broadcasted_iota(jnp.int32, (c, c), 1)
    parts = []
    for g in range(SGU_GROUPS):
        ws = jnp.where(row >= col, ws_ref[g], 0.0).astype(BF16)
        vs = jnp.dot(ws, v[:, g * SGU_GD:(g + 1) * SGU_GD], preferred_element_type=F32)
        parts.append((u[:, g * SGU_GD:(g + 1) * SGU_GD] * (vs + bs_ref[g])).astype(BF16))
    mix = jnp.dot(jnp.concatenate(parts, axis=-1), wout_ref[...], preferred_element_type=F32)
    _finish_mixer(x, mix, lng_ref, lnb_ref, rwt_ref, rb_ref, x1_ref, x1b_ref, comb_ref)


def _sgu_mixer(x, w_in, sgu_g, sgu_b, w_s, b_s, w_out, ln_g, ln_b, rwt, rb):
    t = x.shape[0]
    c = SGU_CHUNK
    return pl.pallas_call(
        _sgu_kernel,
        out_shape=_mixer_out_shapes(t),
        grid=(t // c,),
        in_specs=[pl.BlockSpec((c, D_MODEL), lambda i: (i, 0)),
                  _const_spec(w_in.shape), _const_spec(sgu_g.shape), _const_spec(sgu_b.shape),
                  _const_spec(w_s.shape), _const_spec(b_s.shape), _const_spec(w_out.shape),
                  _const_spec(ln_g.shape), _const_spec(ln_b.shape),
                  _const_spec(rwt.shape), _const_spec(rb.shape)],
        out_specs=_mixer_out_specs(c),
        compiler_params=pltpu.CompilerParams(dimension_semantics=("arbitrary",),
                                             vmem_limit_bytes=VMEM_LIMIT_BYTES),
        name="sgu_mixer",
    )(x, w_in, sgu_g, sgu_b, w_s, b_s, w_out, ln_g, ln_b, rwt, rb)


def _moe_kernel(x_ref, comb_ref, wg_ref, wu_ref, wd_ref, y_ref):
    e = pl.program_id(1)

    @pl.when(e == 0)
    def _():
        y_ref[...] = jnp.zeros_like(y_ref)
    xb = x_ref[...]
    hg = jnp.dot(xb, wg_ref[0], preferred_element_type=F32)
    hu = jnp.dot(xb, wu_ref[0], preferred_element_type=F32)
    he = (jax.nn.silu(hg) * hu).astype(BF16)
    comb = comb_ref[...]
    lane = lax.broadcasted_iota(jnp.int32, comb.shape, 1)
    c = jnp.sum(jnp.where(lane == e, comb, 0.0), axis=1, keepdims=True)
    y_ref[...] += c * jnp.dot(he, wd_ref[0], preferred_element_type=F32)


def _moe(x1b, comb, w_gate, w_up, w_down):
    t = x1b.shape[0]
    tm = TM_MOE
    return pl.pallas_call(
        _moe_kernel,
        out_shape=jax.ShapeDtypeStruct((t, D_MODEL), F32),
        grid=(t // tm, N_EXPERTS),
        in_specs=[pl.BlockSpec((tm, D_MODEL), lambda i, e: (i, 0)),
                  pl.BlockSpec((tm, 128), lambda i, e: (i, 0)),
                  pl.BlockSpec((1, D_MODEL, D_EXPERT), lambda i, e: (e, 0, 0)),
                  pl.BlockSpec((1, D_MODEL, D_EXPERT), lambda i, e: (e, 0, 0)),
                  pl.BlockSpec((1, D_EXPERT, D_MODEL), lambda i, e: (e, 0, 0))],
        out_specs=pl.BlockSpec((tm, D_MODEL), lambda i, e: (i, 0)),
        compiler_params=pltpu.CompilerParams(dimension_semantics=("arbitrary", "arbitrary"),
                                             vmem_limit_bytes=VMEM_LIMIT_BYTES),
        name="moe",
    )(x1b, comb, w_gate, w_up, w_down)


def _post_kernel(x1_ref, y_ref, p_ref, lng_ref, lnb_ref, wp_ref, wg_ref, bg_ref, o_ref):
    x2 = _layer_norm(ALPHA * x1_ref[...] + y_ref[...], lng_ref[...], lnb_ref[...])
    emb = jnp.dot(p_ref[...].astype(BF16), wp_ref[...], preferred_element_type=F32)
    gate = jax.nn.sigmoid(
        jnp.dot(x2.astype(BF16), wg_ref[...], preferred_element_type=F32) + bg_ref[...])
    o_ref[...] = x2 + gate * emb


def _post(x1, y, p, ln_g, ln_b, w_proj, w_gate, b_gate):
    t = x1.shape[0]
    tm = TM_POST
    return pl.pallas_call(
        _post_kernel,
        out_shape=jax.ShapeDtypeStruct((t, D_MODEL), F32),
        grid=(t // tm,),
        in_specs=[pl.BlockSpec((tm, D_MODEL), lambda i: (i, 0)),
                  pl.BlockSpec((tm, D_MODEL), lambda i: (i, 0)),
                  pl.BlockSpec((tm, PLE_DIM), lambda i: (i, 0)),
                  _const_spec(ln_g.shape), _const_spec(ln_b.shape),
                  _const_spec(w_proj.shape), _const_spec(w_gate.shape), _const_spec(b_gate.shape)],
        out_specs=pl.BlockSpec((tm, D_MODEL), lambda i: (i, 0)),
        compiler_params=pltpu.CompilerParams(dimension_semantics=("arbitrary",),
                                             vmem_limit_bytes=VMEM_LIMIT_BYTES),
        name="post_moe",
    )(x1, y, p, ln_g, ln_b, w_proj, w_gate, b_gate)


def kernel(x, p, a_w_in, a_conv_w, a_w_out, b_w_in, b_w_out, c_w_in, c_ln_g, c_ln_b, c_w_s, c_b_s,
           c_w_out, router_w, router_b, moe_w_gate, moe_w_up, moe_w_down, ln_g, ln_b, ple_w_proj,
           ple_w_gate, ple_b_gate):
    bsz, seq, d = x.shape
    t = bsz * seq
    xt = x.reshape(t, d)
    pt = p.reshape(DEPTH, t, PLE_DIM)
    rwt = router_w.T.astype(BF16)
    rb = router_b.astype(F32).reshape(N_EXPERTS, 1)
    row = lambda a: a.reshape(1, -1)
    for i in range(DEPTH):
        mixer, slot = i % N_MIXERS, i // N_MIXERS
        g1, b1 = row(ln_g[i, 0]), row(ln_b[i, 0])
        if mixer == 0:
            x1, x1b, comb = _conv_mixer(xt, seq, a_w_in[slot].astype(BF16), a_conv_w[slot],
                                        a_w_out[slot].astype(BF16), g1, b1, rwt, rb)
        elif mixer == 1:
            x1, x1b, comb = _retention_mixer(xt, seq, b_w_in[slot].astype(BF16),
                                             b_w_out[slot].astype(BF16), g1, b1, rwt, rb)
        else:
            x1, x1b, comb = _sgu_mixer(xt, c_w_in[slot].astype(BF16), row(c_ln_g[slot]),
                                       row(c_ln_b[slot]), c_w_s[slot], c_b_s[slot][:, :, None],
                                       c_w_out[slot].astype(BF16), g1, b1, rwt, rb)
        y = _moe(x1b, comb, moe_w_gate[i].astype(BF16), moe_w_up[i].astype(BF16),
                 moe_w_down[i].astype(BF16))
        xt = _post(x1, y, pt[i], row(ln_g[i, 1]), row(ln_b[i, 1]), ple_w_proj[i].astype(BF16),
                   ple_w_gate[i].astype(BF16), row(ple_b_gate[i]))
    return xt.reshape(bsz, seq, d)
```

```python
import functools
import math

import numpy as np
import jax
import jax.numpy as jnp
from jax import lax
from jax.experimental import pallas as pl
from jax.experimental.pallas import tpu as pltpu

F32 = jnp.float32
BF16 = jnp.bfloat16

D_MODEL = 1024
DEPTH = 4
N_MIXERS = 3
CONV_WIDTH = 3
RET_HEADS = 4
RET_DK = D_MODEL // RET_HEADS
RET_DV = 2 * RET_DK
RET_VDIM = RET_HEADS * RET_DV
RET_CHUNK = 128
ROPE_BASE = 10000.0
SGU_HALF = 3 * D_MODEL
SGU_GROUPS = 4
SGU_GD = SGU_HALF // SGU_GROUPS
SGU_CHUNK = 128
N_EXPERTS = 16
N_GROUPS = 4
EXPERTS_PER_GROUP = N_EXPERTS // N_GROUPS
D_EXPERT = D_MODEL // 2
PLE_DIM = 256
ALPHA = (2 * DEPTH) ** 0.25
LN_EPS = 1e-5

VMEM_LIMIT_BYTES = 56 * 1024 * 1024
SUBLANES = 8
LANES = 128
LANE_TILES = D_MODEL // LANES

TM_CONV = 512
TM_POST = 512
TB_MOE = 2048
CH_MOE = 256


def _const_spec(shape):
    nd = len(shape)
    return pl.BlockSpec(shape, lambda *_: (0,) * nd, pipeline_mode=pl.Buffered(1))


def _layer_norm(h, g, b):
    mu = jnp.mean(h, axis=-1, keepdims=True)
    hc = h - mu
    var = jnp.mean(hc * hc, axis=-1, keepdims=True)
    return hc * lax.rsqrt(var + LN_EPS) * g + b


def _top2_sum(a, b, c, d):
    m1, n1 = jnp.maximum(a, b), jnp.minimum(a, b)
    m2, n2 = jnp.maximum(c, d), jnp.minimum(c, d)
    return jnp.maximum(m1, m2) + jnp.maximum(jnp.minimum(m1, m2), jnp.maximum(n1, n2))


def _route(x1b, rwt_ref, rb_ref):
    tm = x1b.shape[0]
    logits_t = lax.dot_general(rwt_ref[...], x1b, (((1,), (1,)), ((), ())),
                               preferred_element_type=F32)
    scores = jax.nn.sigmoid(logits_t)
    sel = scores + rb_ref[...]
    s = [sel[e:e + 1, :] for e in range(N_EXPERTS)]
    sc = [scores[e:e + 1, :] for e in range(N_EXPERTS)]
    grp = [_top2_sum(*s[4 * g:4 * g + 4]) for g in range(N_GROUPS)]
    best, g_idx = grp[0], jnp.zeros((1, tm), jnp.int32)
    for g in range(1, N_GROUPS):
        better = grp[g] > best
        best = jnp.where(better, grp[g], best)
        g_idx = jnp.where(better, g, g_idx)

    def pick(rows, j):
        out = rows[j]
        for g in range(1, N_GROUPS):
            out = jnp.where(g_idx == g, rows[4 * g + j], out)
        return out
    v = [pick(s, j) for j in range(EXPERTS_PER_GROUP)]
    w = [pick(sc, j) for j in range(EXPERTS_PER_GROUP)]

    def argmax4(vals):
        bv, bi = vals[0], jnp.zeros((1, tm), jnp.int32)
        for j in range(1, EXPERTS_PER_GROUP):
            better = vals[j] > bv
            bv = jnp.where(better, vals[j], bv)
            bi = jnp.where(better, j, bi)
        return bi
    i1 = argmax4(v)
    i2 = argmax4([jnp.where(i1 == j, -jnp.inf, v[j]) for j in range(EXPERTS_PER_GROUP)])

    def take(vals, idx):
        out = vals[0]
        for j in range(1, EXPERTS_PER_GROUP):
            out = jnp.where(idx == j, vals[j], out)
        return out
    w1, w2 = take(w, i1), take(w, i2)
    den = w1 + w2
    w1, w2 = w1 / den, w2 / den
    e1, e2 = g_idx * EXPERTS_PER_GROUP + i1, g_idx * EXPERTS_PER_GROUP + i2
    return e1, e2, w1, w2


def _finish_mixer(x, mix, lng_ref, lnb_ref, rwt_ref, rb_ref, x1_ref, x1lin_ref, eidx_ref, wcol_ref):
    tm = x.shape[0]
    x1 = _layer_norm(ALPHA * x + mix, lng_ref[...], lnb_ref[...])
    x1_ref[...] = x1
    for c in range(LANE_TILES):
        x1lin_ref[pl.ds(c, tm, stride=LANE_TILES), :] = x1[:, c * LANES:(c + 1) * LANES]
    e1, e2, w1, w2 = _route(x1.astype(BF16), rwt_ref, rb_ref)
    eidx_ref[...] = jnp.concatenate([e1, e2], axis=0)
    pad = jnp.zeros((LANES - 2, tm), F32)
    wcol_ref[...] = jnp.concatenate([w1, w2, pad], axis=0).T


def _conv_mixer_kernel(tiles_per_seq, x_ref, win_ref, cw_ref, wout_ref, lng_ref, lnb_ref,
                       rwt_ref, rb_ref, x1_ref, x1lin_ref, eidx_ref, wcol_ref, carry_ref):
    i = pl.program_id(0)
    tm = x_ref.shape[0]
    x = x_ref[...]
    bch = jnp.dot(x.astype(BF16), win_ref[...], preferred_element_type=F32)
    b_gate = bch[:, :D_MODEL]
    z = bch[:, D_MODEL:2 * D_MODEL] * bch[:, 2 * D_MODEL:]

    @pl.when(i % tiles_per_seq == 0)
    def _():
        carry_ref[...] = jnp.zeros_like(carry_ref)
    prev = carry_ref[...]
    row = lax.broadcasted_iota(jnp.int32, (tm, 1), 0)
    p1, p2 = prev[SUBLANES - 1:SUBLANES], prev[SUBLANES - 2:SUBLANES - 1]
    z1 = jnp.where(row == 0, p1, pltpu.roll(z, 1, axis=0))
    z2 = jnp.where(row == 0, p2, jnp.where(row == 1, p1, pltpu.roll(z, 2, axis=0)))
    carry_ref[...] = z[tm - SUBLANES:, :]
    cw = cw_ref[...]
    zc = cw[0:1] * z2 + cw[1:2] * z1 + cw[2:3] * z
    mix = jnp.dot((b_gate * zc).astype(BF16), wout_ref[...], preferred_element_type=F32)
    _finish_mixer(x, mix, lng_ref, lnb_ref, rwt_ref, rb_ref, x1_ref, x1lin_ref, eidx_ref, wcol_ref)


def _mixer_out_shapes(t):
    return (jax.ShapeDtypeStruct((t, D_MODEL), F32),
            jax.ShapeDtypeStruct((t * LANE_TILES, LANES), F32),
            jax.ShapeDtypeStruct((2, t), jnp.int32),
            jax.ShapeDtypeStruct((t, LANES), F32))


def _mixer_out_specs(tm):
    return (pl.BlockSpec((tm, D_MODEL), lambda i: (i, 0)),
            pl.BlockSpec((tm * LANE_TILES, LANES), lambda i: (i, 0)),
            pl.BlockSpec((2, tm), lambda i: (0, i)),
            pl.BlockSpec((tm, LANES), lambda i: (i, 0)))


def _conv_mixer(x, seq, w_in, conv_w, w_out, ln_g, ln_b, rwt, rb):
    t = x.shape[0]
    tm = TM_CONV
    return pl.pallas_call(
        functools.partial(_conv_mixer_kernel, seq // tm),
        out_shape=_mixer_out_shapes(t),
        grid=(t // tm,),
        in_specs=[pl.BlockSpec((tm, D_MODEL), lambda i: (i, 0)),
                  _const_spec(w_in.shape), _const_spec(conv_w.shape), _const_spec(w_out.shape),
                  _const_spec(ln_g.shape), _const_spec(ln_b.shape),
                  _const_spec(rwt.shape), _const_spec(rb.shape)],
        out_specs=_mixer_out_specs(tm),
        scratch_shapes=[pltpu.VMEM((SUBLANES, D_MODEL), F32)],
        compiler_params=pltpu.CompilerParams(dimension_semantics=("arbitrary",),
                                             vmem_limit_bytes=VMEM_LIMIT_BYTES),
        name="conv_mixer",
    )(x, w_in, conv_w, w_out, ln_g, ln_b, rwt, rb)


def _rotary(t, cos, sin):
    half = t.shape[-1] // 2
    t1, t2 = t[:, :half], t[:, half:]
    return jnp.concatenate([t1 * cos - t2 * sin, t2 * cos + t1 * sin], axis=-1)


def _retention_kernel(chunks_per_seq, x_ref, win_ref, wout_ref, cos_ref, sin_ref, dmask_ref,
                      qdec_ref, kdec_ref, cdec_ref, lng_ref, lnb_ref, rwt_ref, rb_ref,
                      x1_ref, x1lin_ref, eidx_ref, wcol_ref, state_ref):
    i = pl.program_id(0)

    @pl.when(i % chunks_per_seq == 0)
    def _():
        state_ref[...] = jnp.zeros_like(state_ref)

    x = x_ref[...]
    proj = jnp.dot(x.astype(BF16), win_ref[...], preferred_element_type=F32)
    cos, sin = cos_ref[...], sin_ref[...]
    gated = []
    for h in range(RET_HEADS):
        q = _rotary(proj[:, h * RET_DK:(h + 1) * RET_DK], cos, sin)
        k = _rotary(proj[:, D_MODEL + h * RET_DK:D_MODEL + (h + 1) * RET_DK] * (RET_DK ** -0.5),
                    cos, sin)
        v = proj[:, 2 * D_MODEL + h * RET_DV:2 * D_MODEL + (h + 1) * RET_DV]
        g = proj[:, 2 * D_MODEL + RET_VDIM + h * RET_DV:2 * D_MODEL + RET_VDIM + (h + 1) * RET_DV]
        qb, vb = q.astype(BF16), v.astype(BF16)
        scores = lax.dot_general(qb, k.astype(BF16), (((1,), (1,)), ((), ())),
                                 preferred_element_type=F32) * dmask_ref[h]
        inner = jnp.dot(scores.astype(BF16), vb, preferred_element_type=F32)
        state = state_ref[h]
        cross = jnp.dot(qb, state.astype(BF16), preferred_element_type=F32) * qdec_ref[h]
        kd = (k * kdec_ref[h]).astype(BF16)
        state_ref[h] = state * cdec_ref[h] + lax.dot_general(
            kd, vb, (((0,), (0,)), ((), ())), preferred_element_type=F32)
        o = inner + cross
        mu = jnp.mean(o, axis=-1, keepdims=True)
        oc = o - mu
        o = oc * lax.rsqrt(jnp.mean(oc * oc, axis=-1, keepdims=True) + LN_EPS)
        gated.append((jax.nn.silu(g) * o).astype(BF16))
    mix = jnp.dot(jnp.concatenate(gated, axis=-1), wout_ref[...], preferred_element_type=F32)
    _finish_mixer(x, mix, lng_ref, lnb_ref, rwt_ref, rb_ref, x1_ref, x1lin_ref, eidx_ref, wcol_ref)


def _retention_tables(seq):
    c = RET_CHUNK
    pos = jnp.arange(seq, dtype=F32)
    inv_freq = 1.0 / (ROPE_BASE ** jnp.linspace(0.0, 1.0, RET_DK // 2, dtype=F32))
    ang = pos[:, None] * inv_freq[None, :]
    log_gamma = jnp.log(1.0 - 2.0 ** (-5.0 - jnp.arange(RET_HEADS, dtype=F32)))
    idx = jnp.arange(c, dtype=F32)
    diff = idx[:, None] - idx[None, :]
    dmask = jnp.where(diff >= 0, jnp.exp(log_gamma[:, None, None] * jnp.maximum(diff, 0.0)), 0.0)
    qdec = jnp.exp(log_gamma[:, None] * (idx[None, :] + 1.0))[:, :, None]
    kdec = jnp.exp(log_gamma[:, None] * (c - 1.0 - idx[None, :]))[:, :, None]
    cdec = jnp.broadcast_to(jnp.exp(log_gamma * c)[:, None, None], (RET_HEADS, 1, RET_DV))
    return jnp.cos(ang), jnp.sin(ang), dmask, qdec, kdec, cdec


def _retention_mixer(x, seq, w_in, w_out, ln_g, ln_b, rwt, rb):
    t = x.shape[0]
    c = RET_CHUNK
    cps = seq // c
    cos, sin, dmask, qdec, kdec, cdec = _retention_tables(seq)
    return pl.pallas_call(
        functools.partial(_retention_kernel, cps),
        out_shape=_mixer_out_shapes(t),
        grid=(t // c,),
        in_specs=[pl.BlockSpec((c, D_MODEL), lambda i: (i, 0)),
                  _const_spec(w_in.shape), _const_spec(w_out.shape),
                  pl.BlockSpec((c, RET_DK // 2), lambda i: (i % cps, 0)),
                  pl.BlockSpec((c, RET_DK // 2), lambda i: (i % cps, 0)),
                  _const_spec(dmask.shape), _const_spec(qdec.shape), _const_spec(kdec.shape),
                  _const_spec(cdec.shape),
                  _const_spec(ln_g.shape), _const_spec(ln_b.shape),
                  _const_spec(rwt.shape), _const_spec(rb.shape)],
        out_specs=_mixer_out_specs(c),
        scratch_shapes=[pltpu.VMEM((RET_HEADS, RET_DK, RET_DV), F32)],
        compiler_params=pltpu.CompilerParams(dimension_semantics=("arbitrary",),
                                             vmem_limit_bytes=VMEM_LIMIT_BYTES),
        name="retention_mixer",
    )(x, w_in, w_out, cos, sin, dmask, qdec, kdec, cdec, ln_g, ln_b, rwt, rb)


def _sgu_kernel(x_ref, win_ref, sg_ref, sb_ref, ws_ref, bs_ref, wout_ref, lng_ref, lnb_ref,
                rwt_ref, rb_ref, x1_ref, x1lin_ref, eidx_ref, wcol_ref):
    c = x_ref.shape[0]
    x = x_ref[...]
    zin = jnp.dot(x.astype(BF16), win_ref[...], preferred_element_type=F32)
    z = 0.5 * zin * (1.0 + lax.erf(zin * (2.0 ** -0.5)))
    u = z[:, :SGU_HALF]
    v = _layer_norm(z[:, SGU_HALF:], sg_ref[...], sb_ref[...]).astype(BF16)
    row = lax.broadcasted_iota(jnp.int32, (c, c), 0)
    col = lax.broadcasted_iota(jnp.int32, (c, c), 1)
    parts = []
    for g in range(SGU_GROUPS):
        ws = jnp.where(row >= col, ws_ref[g], 0.0).astype(BF16)
        vs = jnp.dot(ws, v[:, g * SGU_GD:(g + 1) * SGU_GD], preferred_element_type=F32)
        parts.append((u[:, g * SGU_GD:(g + 1) * SGU_GD] * (vs + bs_ref[g])).astype(BF16))
    mix = jnp.dot(jnp.concatenate(parts, axis=-1), wout_ref[...], preferred_element_type=F32)
    _finish_mixer(x, mix, lng_ref, lnb_ref, rwt_ref, rb_ref, x1_ref, x1lin_ref, eidx_ref, wcol_ref)


def _sgu_mixer(x, w_in, sgu_g, sgu_b, w_s, b_s, w_out, ln_g, ln_b, rwt, rb):
    t = x.shape[0]
    c = SGU_CHUNK
    return pl.pallas_call(
        _sgu_kernel,
        out_shape=_mixer_out_shapes(t),
        grid=(t // c,),
        in_specs=[pl.BlockSpec((c, D_MODEL), lambda i: (i, 0)),
                  _const_spec(w_in.shape), _const_spec(sgu_g.shape), _const_spec(sgu_b.shape),
                  _const_spec(w_s.shape), _const_spec(b_s.shape), _const_spec(w_out.shape),
                  _const_spec(ln_g.shape), _const_spec(ln_b.shape),
                  _const_spec(rwt.shape), _const_spec(rb.shape)],
        out_specs=_mixer_out_specs(c),
        compiler_params=pltpu.CompilerParams(dimension_semantics=("arbitrary",),
                                             vmem_limit_bytes=VMEM_LIMIT_BYTES),
        name="sgu_mixer",
    )(x, w_in, sgu_g, sgu_b, w_s, b_s, w_out, ln_g, ln_b, rwt, rb)


PLAN_BL = 256


def _plan_kernel(eidx_ref, dest_ref, offs_ref):
    tb = eidx_ref.shape[1]
    ns = 2 * tb
    e_all = jnp.concatenate([eidx_ref[0:1, :], eidx_ref[1:2, :]], axis=1)
    eid = lax.broadcasted_iota(jnp.int32, (N_EXPERTS, ns), 0)
    mask = eid == e_all
    off = jnp.sum(jnp.where(e_all < eid, 1.0, 0.0), axis=1, keepdims=True)
    r = lax.broadcasted_iota(jnp.int32, (PLAN_BL, PLAN_BL), 0)
    c = lax.broadcasted_iota(jnp.int32, (PLAN_BL, PLAN_BL), 1)
    upper = jnp.where(r < c, 1.0, 0.0).astype(BF16)
    carry = off
    pieces = []
    for b in range(ns // PLAN_BL):
        mb = mask[:, b * PLAN_BL:(b + 1) * PLAN_BL]
        mbf = jnp.where(mb, 1.0, 0.0)
        pos = jnp.dot(mbf.astype(BF16), upper, preferred_element_type=F32) + carry
        pieces.append(jnp.sum(jnp.where(mb, pos, 0.0), axis=0, keepdims=True))
        carry = carry + jnp.sum(mbf, axis=1, keepdims=True)
    dest_ref[0] = jnp.concatenate(pieces, axis=1).astype(jnp.int32)
    sub = lax.broadcasted_iota(jnp.int32, (N_EXPERTS, LANES), 0)
    lane = lax.broadcasted_iota(jnp.int32, (N_EXPERTS, LANES), 1)
    offs_row = jnp.sum(jnp.where(sub == lane, off, 0.0), axis=0, keepdims=True)
    lane1 = lax.broadcasted_iota(jnp.int32, (1, LANES), 1)
    offs_ref[0] = jnp.where(lane1 == N_EXPERTS, float(ns), offs_row).astype(jnp.int32)


def _plan(eidx, tb):
    t = eidx.shape[1]
    nb = t // tb
    return pl.pallas_call(
        _plan_kernel,
        out_shape=(jax.ShapeDtypeStruct((nb, 1, 2 * tb), jnp.int32),
                   jax.ShapeDtypeStruct((nb, 1, LANES), jnp.int32)),
        grid=(nb,),
        in_specs=[pl.BlockSpec((2, tb), lambda i: (0, i))],
        out_specs=(pl.BlockSpec((1, 1, 2 * tb), lambda i: (i, 0, 0)),
                   pl.BlockSpec((1, 1, LANES), lambda i: (i, 0, 0))),
        compiler_params=pltpu.CompilerParams(dimension_semantics=("arbitrary",)),
        name="moe_plan",
    )(eidx)


def _moe_kernel(dest_ref, offs_ref, z_ref, wcol_ref, wg_ref, wu_ref, wd_ref, y_ref,
                inv_ref, g_ref, o_ref, y2_ref):
    e = pl.program_id(1)
    tb = y_ref.shape[0]
    ns = 2 * tb
    ch = g_ref.shape[0] // LANE_TILES
    row_of = lambda idx: pl.ds(pl.multiple_of(idx * LANE_TILES, LANE_TILES), LANE_TILES)

    @pl.when(e == 0)
    def _():
        def pad_body(i, carry):
            inv_ref[ns + i] = 0
            return carry
        lax.fori_loop(0, ch, pad_body, 0)

        def inv_body(i, carry):
            for u in range(SUBLANES):
                s = i * SUBLANES + u
                inv_ref[dest_ref[0, 0, s]] = s
            return carry
        lax.fori_loop(0, ns // SUBLANES, inv_body, 0)

    start = offs_ref[0, 0, e]
    end = offs_ref[0, 0, e + 1]
    n_chunks = (end - start + (ch - 1)) // ch

    def chunk_body(c, carry):
        a = start + c * ch

        def gather_body(i, carry2):
            for u in range(SUBLANES):
                j = i * SUBLANES + u
                tok = inv_ref[a + j] & (tb - 1)
                g_ref[row_of(j), :] = z_ref[row_of(tok), :]
            return carry2
        lax.fori_loop(0, ch // SUBLANES, gather_body, 0)

        xg = jnp.concatenate([g_ref[pl.ds(cc, ch, stride=LANE_TILES), :]
                              for cc in range(LANE_TILES)], axis=1).astype(BF16)
        hg = jnp.dot(xg, wg_ref[0], preferred_element_type=F32)
        hu = jnp.dot(xg, wu_ref[0], preferred_element_type=F32)
        he = (jax.nn.silu(hg) * hu).astype(BF16)
        o = jnp.dot(he, wd_ref[0], preferred_element_type=F32)
        for cc in range(LANE_TILES):
            o_ref[pl.ds(cc, ch, stride=LANE_TILES), :] = o[:, cc * LANES:(cc + 1) * LANES]

        n_valid = jnp.minimum(ch, end - a)

        def scatter_body(i, carry2):
            for u in range(SUBLANES):
                j = i * SUBLANES + u
                slot = jnp.where(j < n_valid, inv_ref[a + j], ns)
                y2_ref[row_of(slot), :] = o_ref[row_of(j), :]
            return carry2
        lax.fori_loop(0, (n_valid + (SUBLANES - 1)) // SUBLANES, scatter_body, 0)
        return carry
    lax.fori_loop(0, n_chunks, chunk_body, 0)

    @pl.when(e == N_EXPERTS - 1)
    def _():
        w = wcol_ref[...]
        w1, w2 = w[:, 0:1], w[:, 1:2]
        for cc in range(LANE_TILES):
            y_first = y2_ref[pl.ds(cc, tb, stride=LANE_TILES), :]
            y_second = y2_ref[pl.ds(tb * LANE_TILES + cc, tb, stride=LANE_TILES), :]
            y_ref[:, cc * LANES:(cc + 1) * LANES] = w1 * y_first + w2 * y_second


def _moe(x1lin, eidx, wcol, w_gate, w_up, w_down):
    t = wcol.shape[0]
    tb, ch = TB_MOE, CH_MOE
    dest, offs = _plan(eidx, tb)
    smem_spec = lambda n: pl.BlockSpec((1, 1, n), lambda i, e: (i, 0, 0), memory_space=pltpu.SMEM)
    return pl.pallas_call(
        _moe_kernel,
        out_shape=jax.ShapeDtypeStruct((t, D_MODEL), F32),
        grid=(t // tb, N_EXPERTS),
        in_specs=[smem_spec(2 * tb), smem_spec(LANES),
                  pl.BlockSpec((tb * LANE_TILES, LANES), lambda i, e: (i, 0),
                               pipeline_mode=pl.Buffered(1)),
                  pl.BlockSpec((tb, LANES), lambda i, e: (i, 0)),
                  pl.BlockSpec((1, D_MODEL, D_EXPERT), lambda i, e: (e, 0, 0)),
                  pl.BlockSpec((1, D_MODEL, D_EXPERT), lambda i, e: (e, 0, 0)),
                  pl.BlockSpec((1, D_EXPERT, D_MODEL), lambda i, e: (e, 0, 0))],
        out_specs=pl.BlockSpec((tb, D_MODEL), lambda i, e: (i, 0)),
        scratch_shapes=[pltpu.SMEM((2 * tb + ch,), jnp.int32),
                        pltpu.VMEM((ch * LANE_TILES, LANES), F32),
                        pltpu.VMEM((ch * LANE_TILES, LANES), F32),
                        pltpu.VMEM(((2 * tb + 1) * LANE_TILES, LANES), F32)],
        compiler_params=pltpu.CompilerParams(dimension_semantics=("arbitrary", "arbitrary"),
                                             vmem_limit_bytes=VMEM_LIMIT_BYTES),
        name="moe",
    )(dest, offs, x1lin, wcol, w_gate, w_up, w_down)


def _post_kernel(x1_ref, y_ref, p_ref, lng_ref, lnb_ref, wp_ref, wg_ref, bg_ref, o_ref):
    x2 = _layer_norm(ALPHA * x1_ref[...] + y_ref[...], lng_ref[...], lnb_ref[...])
    emb = jnp.dot(p_ref[...].astype(BF16), wp_ref[...], preferred_element_type=F32)
    gate = jax.nn.sigmoid(
        jnp.dot(x2.astype(BF16), wg_ref[...], preferred_element_type=F32) + bg_ref[...])
    o_ref[...] = x2 + gate * emb


def _post(x1, y, p, ln_g, ln_b, w_proj, w_gate, b_gate):
    t = x1.shape[0]
    tm = TM_POST
    return pl.pallas_call(
        _post_kernel,
        out_shape=jax.ShapeDtypeStruct((t, D_MODEL), F32),
        grid=(t // tm,),
        in_specs=[pl.BlockSpec((tm, D_MODEL), lambda i: (i, 0)),
                  pl.BlockSpec((tm, D_MODEL), lambda i: (i, 0)),
                  pl.BlockSpec((tm, PLE_DIM), lambda i: (i, 0)),
                  _const_spec(ln_g.shape), _const_spec(ln_b.shape),
                  _const_spec(w_proj.shape), _const_spec(w_gate.shape), _const_spec(b_gate.shape)],
        out_specs=pl.BlockSpec((tm, D_MODEL), lambda i: (i, 0)),
        compiler_params=pltpu.CompilerParams(dimension_semantics=("arbitrary",),
                                             vmem_limit_bytes=VMEM_LIMIT_BYTES),
        name="post_moe",
    )(x1, y, p, ln_g, ln_b, w_proj, w_gate, b_gate)


def kernel(x, p, a_w_in, a_conv_w, a_w_out, b_w_in, b_w_out, c_w_in, c_ln_g, c_ln_b, c_w_s, c_b_s,
           c_w_out, router_w, router_b, moe_w_gate, moe_w_up, moe_w_down, ln_g, ln_b, ple_w_proj,
           ple_w_gate, ple_b_gate):
    bsz, seq, d = x.shape
    t = bsz * seq
    xt = x.reshape(t, d)
    pt = p.reshape(DEPTH, t, PLE_DIM)
    rwt = router_w.T.astype(BF16)
    rb = router_b.astype(F32).reshape(N_EXPERTS, 1)
    row = lambda a: a.reshape(1, -1)
    for i in range(DEPTH):
        mixer, slot = i % N_MIXERS, i // N_MIXERS
        g1, b1 = row(ln_g[i, 0]), row(ln_b[i, 0])
        if mixer == 0:
            outs = _conv_mixer(xt, seq, a_w_in[slot].astype(BF16), a_conv_w[slot],
                               a_w_out[slot].astype(BF16), g1, b1, rwt, rb)
        elif mixer == 1:
            outs = _retention_mixer(xt, seq, b_w_in[slot].astype(BF16),
                                    b_w_out[slot].astype(BF16), g1, b1, rwt, rb)
        else:
            outs = _sgu_mixer(xt, c_w_in[slot].astype(BF16), row(c_ln_g[slot]),
                              row(c_ln_b[slot]), c_w_s[slot], c_b_s[slot][:, :, None],
                              c_w_out[slot].astype(BF16), g1, b1, rwt, rb)
        x1, x1lin, eidx, wcol = outs
        y = _moe(x1lin, eidx, wcol, moe_w_gate[i].astype(BF16), moe_w_up[i].astype(BF16),
                 moe_w_down[i].astype(BF16))
        xt = _post(x1, y, pt[i], row(ln_g[i, 1]), row(ln_b[i, 1]), ple_w_proj[i].astype(BF16),
                   ple_w_gate[i].astype(BF16), row(ple_b_gate[i]))
    return xt.reshape(bsz, seq, d)
```

```python
import functools
import math

import numpy as np
import jax
import jax.numpy as jnp
from jax import lax
from jax.experimental import pallas as pl
from jax.experimental.pallas import tpu as pltpu

F32 = jnp.float32
BF16 = jnp.bfloat16

D_MODEL = 1024
DEPTH = 4
N_MIXERS = 3
CONV_WIDTH = 3
RET_HEADS = 4
RET_DK = D_MODEL // RET_HEADS
RET_DV = 2 * RET_DK
RET_VDIM = RET_HEADS * RET_DV
RET_CHUNK = 128
ROPE_BASE = 10000.0
SGU_HALF = 3 * D_MODEL
SGU_GROUPS = 4
SGU_GD = SGU_HALF // SGU_GROUPS
SGU_CHUNK = 128
N_EXPERTS = 16
N_GROUPS = 4
EXPERTS_PER_GROUP = N_EXPERTS // N_GROUPS
D_EXPERT = D_MODEL // 2
PLE_DIM = 256
ALPHA = (2 * DEPTH) ** 0.25
LN_EPS = 1e-5

VMEM_LIMIT_BYTES = 56 * 1024 * 1024
SUBLANES = 8
LANES = 128
LANE_TILES = D_MODEL // LANES

TM_CONV = 512
SUB_CONV = 512
TM_RET = 256
TM_SGU = 256
TM_POST = 512
TB_MOE = 2048
CH_MOE = 320


def _const_spec(shape):
    nd = len(shape)
    return pl.BlockSpec(shape, lambda *_: (0,) * nd, pipeline_mode=pl.Buffered(1))


def _slot_spec(shape, slot):
    nd = len(shape)
    return pl.BlockSpec((1,) + tuple(shape[1:]), lambda *_: (slot,) + (0,) * (nd - 1),
                        pipeline_mode=pl.Buffered(1))


def _layer_norm(h, g, b):
    mu = jnp.mean(h, axis=-1, keepdims=True)
    hc = h - mu
    var = jnp.mean(hc * hc, axis=-1, keepdims=True)
    return hc * lax.rsqrt(var + LN_EPS) * g + b


def _top2_sum(a, b, c, d):
    m1, n1 = jnp.maximum(a, b), jnp.minimum(a, b)
    m2, n2 = jnp.maximum(c, d), jnp.minimum(c, d)
    return jnp.maximum(m1, m2) + jnp.maximum(jnp.minimum(m1, m2), jnp.maximum(n1, n2))


def _route(x1b, rwt_ref, rb_ref):
    tm = x1b.shape[0]
    logits_t = lax.dot_general(rwt_ref[...], x1b, (((1,), (1,)), ((), ())),
                               preferred_element_type=F32)
    scores = jax.nn.sigmoid(logits_t)
    sel = scores + rb_ref[...]
    s = [sel[e:e + 1, :] for e in range(N_EXPERTS)]
    sc = [scores[e:e + 1, :] for e in range(N_EXPERTS)]
    grp = [_top2_sum(*s[4 * g:4 * g + 4]) for g in range(N_GROUPS)]
    best, g_idx = grp[0], jnp.zeros((1, tm), jnp.int32)
    for g in range(1, N_GROUPS):
        better = grp[g] > best
        best = jnp.where(better, grp[g], best)
        g_idx = jnp.where(better, g, g_idx)

    def pick(rows, j):
        out = rows[j]
        for g in range(1, N_GROUPS):
            out = jnp.where(g_idx == g, rows[4 * g + j], out)
        return out
    v = [pick(s, j) for j in range(EXPERTS_PER_GROUP)]
    w = [pick(sc, j) for j in range(EXPERTS_PER_GROUP)]

    def argmax4(vals):
        bv, bi = vals[0], jnp.zeros((1, tm), jnp.int32)
        for j in range(1, EXPERTS_PER_GROUP):
            better = vals[j] > bv
            bv = jnp.where(better, vals[j], bv)
            bi = jnp.where(better, j, bi)
        return bi
    i1 = argmax4(v)
    i2 = argmax4([jnp.where(i1 == j, -jnp.inf, v[j]) for j in range(EXPERTS_PER_GROUP)])

    def take(vals, idx):
        out = vals[0]
        for j in range(1, EXPERTS_PER_GROUP):
            out = jnp.where(idx == j, vals[j], out)
        return out
    w1, w2 = take(w, i1), take(w, i2)
    den = w1 + w2
    w1, w2 = w1 / den, w2 / den
    e1, e2 = g_idx * EXPERTS_PER_GROUP + i1, g_idx * EXPERTS_PER_GROUP + i2
    return e1, e2, w1, w2


def _finish_mixer(x, mix, r0, lng_ref, lnb_ref, rwt_ref, rb_ref, x1_ref, x1lin_ref, eidx_ref,
                  wcol_ref):
    tm = x.shape[0]
    x1 = _layer_norm(ALPHA * x + mix, lng_ref[...], lnb_ref[...])
    x1_ref[r0:r0 + tm, :] = x1
    for c in range(LANE_TILES):
        x1lin_ref[pl.ds(r0 * LANE_TILES + c, tm, stride=LANE_TILES), :] = (
            x1[:, c * LANES:(c + 1) * LANES])
    e1, e2, w1, w2 = _route(x1.astype(BF16), rwt_ref, rb_ref)
    eidx_ref[:, r0:r0 + tm] = jnp.concatenate([e1, e2], axis=0)
    pad = jnp.zeros((LANES - 2, tm), F32)
    wcol_ref[r0:r0 + tm, :] = jnp.concatenate([w1, w2, pad], axis=0).T


def _conv_mixer_kernel(tiles_per_seq, x_ref, win_ref, cw_ref, wout_ref, lng_ref, lnb_ref,
                       rwt_ref, rb_ref, x1_ref, x1lin_ref, eidx_ref, wcol_ref, carry_ref):
    i = pl.program_id(0)
    ts = SUB_CONV

    @pl.when(i % tiles_per_seq == 0)
    def _():
        carry_ref[...] = jnp.zeros_like(carry_ref)
    prev = carry_ref[...]
    row = lax.broadcasted_iota(jnp.int32, (ts, 1), 0)
    cw = cw_ref[...]
    for r0 in range(0, x_ref.shape[0], ts):
        x = x_ref[r0:r0 + ts, :]
        bch = jnp.dot(x.astype(BF16), win_ref[0], preferred_element_type=F32)
        b_gate = bch[:, :D_MODEL]
        z = bch[:, D_MODEL:2 * D_MODEL] * bch[:, 2 * D_MODEL:]
        p1, p2 = prev[SUBLANES - 1:SUBLANES], prev[SUBLANES - 2:SUBLANES - 1]
        z1 = jnp.where(row == 0, p1, pltpu.roll(z, 1, axis=0))
        z2 = jnp.where(row == 0, p2, jnp.where(row == 1, p1, pltpu.roll(z, 2, axis=0)))
        prev = z[ts - SUBLANES:, :]
        zc = cw[0:1] * z2 + cw[1:2] * z1 + cw[2:3] * z
        mix = jnp.dot((b_gate * zc).astype(BF16), wout_ref[0], preferred_element_type=F32)
        _finish_mixer(x, mix, r0, lng_ref, lnb_ref, rwt_ref, rb_ref, x1_ref, x1lin_ref, eidx_ref,
                      wcol_ref)
    carry_ref[...] = prev


def _mixer_out_shapes(t):
    return (jax.ShapeDtypeStruct((t, D_MODEL), F32),
            jax.ShapeDtypeStruct((t * LANE_TILES, LANES), F32),
            jax.ShapeDtypeStruct((2, t), jnp.int32),
            jax.ShapeDtypeStruct((t, LANES), F32))


def _mixer_out_specs(tm):
    return (pl.BlockSpec((tm, D_MODEL), lambda i: (i, 0)),
            pl.BlockSpec((tm * LANE_TILES, LANES), lambda i: (i, 0)),
            pl.BlockSpec((2, tm), lambda i: (0, i)),
            pl.BlockSpec((tm, LANES), lambda i: (i, 0)))


def _conv_mixer(x, seq, slot, w_in, conv_w, w_out, ln_g, ln_b, rwt, rb):
    t = x.shape[0]
    tm = TM_CONV
    return pl.pallas_call(
        functools.partial(_conv_mixer_kernel, seq // tm),
        out_shape=_mixer_out_shapes(t),
        grid=(t // tm,),
        in_specs=[pl.BlockSpec((tm, D_MODEL), lambda i: (i, 0)),
                  _slot_spec(w_in.shape, slot), _const_spec(conv_w.shape),
                  _slot_spec(w_out.shape, slot),
                  _const_spec(ln_g.shape), _const_spec(ln_b.shape),
                  _const_spec(rwt.shape), _const_spec(rb.shape)],
        out_specs=_mixer_out_specs(tm),
        scratch_shapes=[pltpu.VMEM((SUBLANES, D_MODEL), F32)],
        compiler_params=pltpu.CompilerParams(dimension_semantics=("arbitrary",),
                                             vmem_limit_bytes=VMEM_LIMIT_BYTES),
        name="conv_mixer",
    )(x, w_in, conv_w, w_out, ln_g, ln_b, rwt, rb)


def _rotary(t, cos, sin):
    half = t.shape[-1] // 2
    t1, t2 = t[:, :half], t[:, half:]
    return jnp.concatenate([t1 * cos - t2 * sin, t2 * cos + t1 * sin], axis=-1)


def _retention_kernel(chunks_per_seq, x_ref, win_ref, wout_ref, cos_ref, sin_ref, dmask_ref,
                      qdec_ref, kdec_ref, cdec_ref, lng_ref, lnb_ref, rwt_ref, rb_ref,
                      x1_ref, x1lin_ref, eidx_ref, wcol_ref, state_ref):
    i = pl.program_id(0)

    @pl.when(i % chunks_per_seq == 0)
    def _():
        state_ref[...] = jnp.zeros_like(state_ref)

    for r0 in range(0, x_ref.shape[0], RET_CHUNK):
        _retention_chunk(r0, x_ref, win_ref, wout_ref, cos_ref, sin_ref, dmask_ref, qdec_ref,
                         kdec_ref, cdec_ref, lng_ref, lnb_ref, rwt_ref, rb_ref, x1_ref, x1lin_ref,
                         eidx_ref, wcol_ref, state_ref)


def _retention_chunk(r0, x_ref, win_ref, wout_ref, cos_ref, sin_ref, dmask_ref, qdec_ref, kdec_ref,
                     cdec_ref, lng_ref, lnb_ref, rwt_ref, rb_ref, x1_ref, x1lin_ref, eidx_ref,
                     wcol_ref, state_ref):
    rows = slice(r0, r0 + RET_CHUNK)
    x = x_ref[rows, :]
    proj = jnp.dot(x.astype(BF16), win_ref[0], preferred_element_type=F32)
    cos, sin = cos_ref[rows, :], sin_ref[rows, :]
    gated = []
    for h in range(RET_HEADS):
        q = _rotary(proj[:, h * RET_DK:(h + 1) * RET_DK], cos, sin)
        k = _rotary(proj[:, D_MODEL + h * RET_DK:D_MODEL + (h + 1) * RET_DK] * (RET_DK ** -0.5),
                    cos, sin)
        v = proj[:, 2 * D_MODEL + h * RET_DV:2 * D_MODEL + (h + 1) * RET_DV]
        g = proj[:, 2 * D_MODEL + RET_VDIM + h * RET_DV:2 * D_MODEL + RET_VDIM + (h + 1) * RET_DV]
        qb, vb = q.astype(BF16), v.astype(BF16)
        scores = lax.dot_general(qb, k.astype(BF16), (((1,), (1,)), ((), ())),
                                 preferred_element_type=F32) * dmask_ref[h]
        inner = jnp.dot(scores.astype(BF16), vb, preferred_element_type=F32)
        state = state_ref[h]
        cross = jnp.dot(qb, state.astype(BF16), preferred_element_type=F32) * qdec_ref[h]
        kd = (k * kdec_ref[h]).astype(BF16)
        state_ref[h] = state * cdec_ref[h] + lax.dot_general(
            kd, vb, (((0,), (0,)), ((), ())), preferred_element_type=F32)
        o = inner + cross
        mu = jnp.mean(o, axis=-1, keepdims=True)
        oc = o - mu
        o = oc * lax.rsqrt(jnp.mean(oc * oc, axis=-1, keepdims=True) + LN_EPS)
        gated.append((jax.nn.silu(g) * o).astype(BF16))
    mix = jnp.dot(jnp.concatenate(gated, axis=-1), wout_ref[0], preferred_element_type=F32)
    _finish_mixer(x, mix, r0, lng_ref, lnb_ref, rwt_ref, rb_ref, x1_ref, x1lin_ref, eidx_ref,
                  wcol_ref)


def _retention_tables(seq):
    c = RET_CHUNK
    pos = jnp.arange(seq, dtype=F32)
    inv_freq = 1.0 / (ROPE_BASE ** jnp.linspace(0.0, 1.0, RET_DK // 2, dtype=F32))
    ang = pos[:, None] * inv_freq[None, :]
    log_gamma = jnp.log(1.0 - 2.0 ** (-5.0 - jnp.arange(RET_HEADS, dtype=F32)))
    idx = jnp.arange(c, dtype=F32)
    diff = idx[:, None] - idx[None, :]
    dmask = jnp.where(diff >= 0, jnp.exp(log_gamma[:, None, None] * jnp.maximum(diff, 0.0)), 0.0)
    qdec = jnp.exp(log_gamma[:, None] * (idx[None, :] + 1.0))[:, :, None]
    kdec = jnp.exp(log_gamma[:, None] * (c - 1.0 - idx[None, :]))[:, :, None]
    cdec = jnp.broadcast_to(jnp.exp(log_gamma * c)[:, None, None], (RET_HEADS, 1, RET_DV))
    return jnp.cos(ang), jnp.sin(ang), dmask, qdec, kdec, cdec


def _retention_mixer(x, seq, slot, w_in, w_out, ln_g, ln_b, rwt, rb):
    t = x.shape[0]
    c = TM_RET
    cps = seq // c
    cos, sin, dmask, qdec, kdec, cdec = _retention_tables(seq)
    return pl.pallas_call(
        functools.partial(_retention_kernel, cps),
        out_shape=_mixer_out_shapes(t),
        grid=(t // c,),
        in_specs=[pl.BlockSpec((c, D_MODEL), lambda i: (i, 0)),
                  _slot_spec(w_in.shape, slot), _slot_spec(w_out.shape, slot),
                  pl.BlockSpec((c, RET_DK // 2), lambda i: (i % cps, 0)),
                  pl.BlockSpec((c, RET_DK // 2), lambda i: (i % cps, 0)),
                  _const_spec(dmask.shape), _const_spec(qdec.shape), _const_spec(kdec.shape),
                  _const_spec(cdec.shape),
                  _const_spec(ln_g.shape), _const_spec(ln_b.shape),
                  _const_spec(rwt.shape), _const_spec(rb.shape)],
        out_specs=_mixer_out_specs(c),
        scratch_shapes=[pltpu.VMEM((RET_HEADS, RET_DK, RET_DV), F32)],
        compiler_params=pltpu.CompilerParams(dimension_semantics=("arbitrary",),
                                             vmem_limit_bytes=VMEM_LIMIT_BYTES),
        name="retention_mixer",
    )(x, w_in, w_out, cos, sin, dmask, qdec, kdec, cdec, ln_g, ln_b, rwt, rb)


def _sgu_kernel(x_ref, win_ref, sg_ref, sb_ref, ws_ref, bs_ref, wout_ref, lng_ref, lnb_ref,
                rwt_ref, rb_ref, x1_ref, x1lin_ref, eidx_ref, wcol_ref):
    c = SGU_CHUNK
    row = lax.broadcasted_iota(jnp.int32, (c, c), 0)
    col = lax.broadcasted_iota(jnp.int32, (c, c), 1)
    ws = [jnp.where(row >= col, ws_ref[g], 0.0).astype(BF16) for g in range(SGU_GROUPS)]
    for r0 in range(0, x_ref.shape[0], c):
        x = x_ref[r0:r0 + c, :]
        zin = jnp.dot(x.astype(BF16), win_ref[0], preferred_element_type=F32)
        z = 0.5 * zin * (1.0 + lax.erf(zin * (2.0 ** -0.5)))
        u = z[:, :SGU_HALF]
        v = _layer_norm(z[:, SGU_HALF:], sg_ref[...], sb_ref[...]).astype(BF16)
        parts = []
        for g in range(SGU_GROUPS):
            vs = jnp.dot(ws[g], v[:, g * SGU_GD:(g + 1) * SGU_GD], preferred_element_type=F32)
            parts.append((u[:, g * SGU_GD:(g + 1) * SGU_GD] * (vs + bs_ref[g])).astype(BF16))
        mix = jnp.dot(jnp.concatenate(parts, axis=-1), wout_ref[0], preferred_element_type=F32)
        _finish_mixer(x, mix, r0, lng_ref, lnb_ref, rwt_ref, rb_ref, x1_ref, x1lin_ref, eidx_ref,
                      wcol_ref)


def _sgu_mixer(x, slot, w_in, sgu_g, sgu_b, w_s, b_s, w_out, ln_g, ln_b, rwt, rb):
    t = x.shape[0]
    c = TM_SGU
    return pl.pallas_call(
        _sgu_kernel,
        out_shape=_mixer_out_shapes(t),
        grid=(t // c,),
        in_specs=[pl.BlockSpec((c, D_MODEL), lambda i: (i, 0)),
                  _slot_spec(w_in.shape, slot), _const_spec(sgu_g.shape), _const_spec(sgu_b.shape),
                  _const_spec(w_s.shape), _const_spec(b_s.shape), _slot_spec(w_out.shape, slot),
                  _const_spec(ln_g.shape), _const_spec(ln_b.shape),
                  _const_spec(rwt.shape), _const_spec(rb.shape)],
        out_specs=_mixer_out_specs(c),
        compiler_params=pltpu.CompilerParams(dimension_semantics=("arbitrary",),
                                             vmem_limit_bytes=VMEM_LIMIT_BYTES),
        name="sgu_mixer",
    )(x, w_in, sgu_g, sgu_b, w_s, b_s, w_out, ln_g, ln_b, rwt, rb)


PLAN_BL = 256


def _plan_kernel(eidx_ref, dest_ref, offs_ref):
    tb = eidx_ref.shape[1]
    ns = 2 * tb
    e_all = jnp.concatenate([eidx_ref[0:1, :], eidx_ref[1:2, :]], axis=1)
    eid = lax.broadcasted_iota(jnp.int32, (N_EXPERTS, ns), 0)
    mask = eid == e_all
    off = jnp.sum(jnp.where(e_all < eid, 1.0, 0.0), axis=1, keepdims=True)
    r = lax.broadcasted_iota(jnp.int32, (PLAN_BL, PLAN_BL), 0)
    c = lax.broadcasted_iota(jnp.int32, (PLAN_BL, PLAN_BL), 1)
    upper = jnp.where(r < c, 1.0, 0.0).astype(BF16)
    carry = off
    pieces = []
    for b in range(ns // PLAN_BL):
        mb = mask[:, b * PLAN_BL:(b + 1) * PLAN_BL]
        mbf = jnp.where(mb, 1.0, 0.0)
        pos = jnp.dot(mbf.astype(BF16), upper, preferred_element_type=F32) + carry
        pieces.append(jnp.sum(jnp.where(mb, pos, 0.0), axis=0, keepdims=True))
        carry = carry + jnp.sum(mbf, axis=1, keepdims=True)
    dest_ref[0] = jnp.concatenate(pieces, axis=1).astype(jnp.int32)
    sub = lax.broadcasted_iota(jnp.int32, (N_EXPERTS, LANES), 0)
    lane = lax.broadcasted_iota(jnp.int32, (N_EXPERTS, LANES), 1)
    offs_row = jnp.sum(jnp.where(sub == lane, off, 0.0), axis=0, keepdims=True)
    lane1 = lax.broadcasted_iota(jnp.int32, (1, LANES), 1)
    offs_ref[0] = jnp.where(lane1 == N_EXPERTS, float(ns), offs_row).astype(jnp.int32)


def _plan(eidx, tb):
    t = eidx.shape[1]
    nb = t // tb
    return pl.pallas_call(
        _plan_kernel,
        out_shape=(jax.ShapeDtypeStruct((nb, 1, 2 * tb), jnp.int32),
                   jax.ShapeDtypeStruct((nb, 1, LANES), jnp.int32)),
        grid=(nb,),
        in_specs=[pl.BlockSpec((2, tb), lambda i: (0, i))],
        out_specs=(pl.BlockSpec((1, 1, 2 * tb), lambda i: (i, 0, 0)),
                   pl.BlockSpec((1, 1, LANES), lambda i: (i, 0, 0))),
        compiler_params=pltpu.CompilerParams(dimension_semantics=("arbitrary",)),
        name="moe_plan",
    )(eidx)


def _moe_kernel(dest_ref, offs_ref, z_ref, wcol_ref, wg_ref, wu_ref, wd_ref, y_ref,
                inv_ref, g_ref, o_ref, y2_ref):
    i = pl.program_id(0)
    e = pl.program_id(1)
    tb = y_ref.shape[0]
    ns = 2 * tb
    ch = g_ref.shape[0] // LANE_TILES
    rows_at = lambda off: pl.ds(pl.multiple_of(off, LANE_TILES), LANE_TILES)

    @pl.when((i == 0) & (e == 0))
    def _():
        g_ref[...] = jnp.zeros_like(g_ref)

    @pl.when(e == 0)
    def _():
        def pad_body(k, carry):
            inv_ref[ns + k] = ns * LANE_TILES
            return carry
        lax.fori_loop(0, SUBLANES, pad_body, 0)

        def inv_body(k, carry):
            for u in range(SUBLANES):
                s = k * SUBLANES + u
                inv_ref[dest_ref[0, 0, s]] = s * LANE_TILES
            return carry
        lax.fori_loop(0, ns // SUBLANES, inv_body, 0)

    start = offs_ref[0, 0, e]
    end = offs_ref[0, 0, e + 1]
    n_chunks = (end - start + (ch - 1)) // ch

    def chunk_body(c, carry):
        a = start + c * ch
        n_groups = (jnp.minimum(ch, end - a) + (SUBLANES - 1)) // SUBLANES

        def gather_body(k, carry2):
            for u in range(SUBLANES):
                j = k * SUBLANES + u
                tok_off = inv_ref[a + j] & (tb * LANE_TILES - 1)
                g_ref[rows_at(j * LANE_TILES), :] = z_ref[rows_at(tok_off), :]
            return carry2
        lax.fori_loop(0, n_groups, gather_body, 0)

        xg = jnp.concatenate([g_ref[pl.ds(cc, ch, stride=LANE_TILES), :]
                              for cc in range(LANE_TILES)], axis=1).astype(BF16)
        hg = jnp.dot(xg, wg_ref[0], preferred_element_type=F32)
        hu = jnp.dot(xg, wu_ref[0], preferred_element_type=F32)
        he = (jax.nn.silu(hg) * hu).astype(BF16)
        o = jnp.dot(he, wd_ref[0], preferred_element_type=F32)
        for cc in range(LANE_TILES):
            o_ref[pl.ds(cc, ch, stride=LANE_TILES), :] = o[:, cc * LANES:(cc + 1) * LANES]

        def scatter_body(k, carry2):
            for u in range(SUBLANES):
                j = k * SUBLANES + u
                y2_ref[rows_at(inv_ref[a + j]), :] = o_ref[rows_at(j * LANE_TILES), :]
            return carry2
        lax.fori_loop(0, n_groups, scatter_body, 0)
        return carry
    lax.fori_loop(0, n_chunks, chunk_body, 0)

    @pl.when(e == N_EXPERTS - 1)
    def _():
        w = wcol_ref[...]
        w1, w2 = w[:, 0:1], w[:, 1:2]
        for cc in range(LANE_TILES):
            y_first = y2_ref[pl.ds(cc, tb, stride=LANE_TILES), :]
            y_second = y2_ref[pl.ds(tb * LANE_TILES + cc, tb, stride=LANE_TILES), :]
            y_ref[:, cc * LANES:(cc + 1) * LANES] = w1 * y_first + w2 * y_second


def _moe(x1lin, eidx, wcol, layer, w_gate, w_up, w_down):
    t = wcol.shape[0]
    tb, ch = TB_MOE, CH_MOE
    dest, offs = _plan(eidx, tb)
    smem_spec = lambda n: pl.BlockSpec((1, 1, n), lambda i, e: (i, 0, 0), memory_space=pltpu.SMEM)
    expert_block = lambda i, e: (layer * N_EXPERTS + e, 0, 0)
    return pl.pallas_call(
        _moe_kernel,
        out_shape=jax.ShapeDtypeStruct((t, D_MODEL), F32),
        grid=(t // tb, N_EXPERTS),
        in_specs=[smem_spec(2 * tb), smem_spec(LANES),
                  pl.BlockSpec((tb * LANE_TILES, LANES), lambda i, e: (i, 0),
                               pipeline_mode=pl.Buffered(1)),
                  pl.BlockSpec((tb, LANES), lambda i, e: (i, 0)),
                  pl.BlockSpec((1, D_MODEL, D_EXPERT), expert_block),
                  pl.BlockSpec((1, D_MODEL, D_EXPERT), expert_block),
                  pl.BlockSpec((1, D_EXPERT, D_MODEL), expert_block)],
        out_specs=pl.BlockSpec((tb, D_MODEL), lambda i, e: (i, 0)),
        scratch_shapes=[pltpu.SMEM((2 * tb + ch,), jnp.int32),
                        pltpu.VMEM((ch * LANE_TILES, LANES), F32),
                        pltpu.VMEM((ch * LANE_TILES, LANES), F32),
                        pltpu.VMEM(((2 * tb + 1) * LANE_TILES, LANES), F32)],
        compiler_params=pltpu.CompilerParams(dimension_semantics=("arbitrary", "arbitrary"),
                                             vmem_limit_bytes=VMEM_LIMIT_BYTES),
        name="moe",
    )(dest, offs, x1lin, wcol, w_gate, w_up, w_down)


def _post_kernel(x1_ref, y_ref, p_ref, lng_ref, lnb_ref, wp_ref, wg_ref, bg_ref, o_ref):
    x2 = _layer_norm(ALPHA * x1_ref[...] + y_ref[...], lng_ref[...], lnb_ref[...])
    emb = jnp.dot(p_ref[0].astype(BF16), wp_ref[0], preferred_element_type=F32)
    gate = jax.nn.sigmoid(
        jnp.dot(x2.astype(BF16), wg_ref[0], preferred_element_type=F32) + bg_ref[...])
    o_ref[...] = x2 + gate * emb


def _post(x1, y, layer, p, ln_g, ln_b, w_proj, w_gate, b_gate):
    t = x1.shape[0]
    tm = TM_POST
    return pl.pallas_call(
        _post_kernel,
        out_shape=jax.ShapeDtypeStruct((t, D_MODEL), F32),
        grid=(t // tm,),
        in_specs=[pl.BlockSpec((tm, D_MODEL), lambda i: (i, 0)),
                  pl.BlockSpec((tm, D_MODEL), lambda i: (i, 0)),
                  pl.BlockSpec((1, tm, PLE_DIM), lambda i: (layer, i, 0)),
                  _const_spec(ln_g.shape), _const_spec(ln_b.shape),
                  _slot_spec(w_proj.shape, layer), _slot_spec(w_gate.shape, layer),
                  _const_spec(b_gate.shape)],
        out_specs=pl.BlockSpec((tm, D_MODEL), lambda i: (i, 0)),
        compiler_params=pltpu.CompilerParams(dimension_semantics=("arbitrary",),
                                             vmem_limit_bytes=VMEM_LIMIT_BYTES),
        name="post_moe",
    )(x1, y, p, ln_g, ln_b, w_proj, w_gate, b_gate)


def kernel(x, p, a_w_in, a_conv_w, a_w_out, b_w_in, b_w_out, c_w_in, c_ln_g, c_ln_b, c_w_s, c_b_s,
           c_w_out, router_w, router_b, moe_w_gate, moe_w_up, moe_w_down, ln_g, ln_b, ple_w_proj,
           ple_w_gate, ple_b_gate):
    bsz, seq, d = x.shape
    t = bsz * seq
    xt = x.reshape(t, d)
    pt = p.reshape(DEPTH, t, PLE_DIM)
    rwt = router_w.T.astype(BF16)
    rb = router_b.astype(F32).reshape(N_EXPERTS, 1)
    row = lambda a: a.reshape(1, -1)
    a_w_in_b, a_w_out_b = a_w_in.astype(BF16), a_w_out.astype(BF16)
    moe_shape = lambda w: w.astype(BF16).reshape((DEPTH * N_EXPERTS,) + w.shape[2:])
    moe_wg, moe_wu, moe_wd = moe_shape(moe_w_gate), moe_shape(moe_w_up), moe_shape(moe_w_down)
    ple_wp, ple_wg = ple_w_proj.astype(BF16), ple_w_gate.astype(BF16)
    for i in range(DEPTH):
        mixer, slot = i % N_MIXERS, i // N_MIXERS
        g1, b1 = row(ln_g[i, 0]), row(ln_b[i, 0])
        if mixer == 0:
            outs = _conv_mixer(xt, seq, slot, a_w_in_b, a_conv_w[slot], a_w_out_b, g1, b1, rwt, rb)
        elif mixer == 1:
            outs = _retention_mixer(xt, seq, slot, b_w_in.astype(BF16), b_w_out.astype(BF16), g1,
                                    b1, rwt, rb)
        else:
            outs = _sgu_mixer(xt, slot, c_w_in.astype(BF16), row(c_ln_g[slot]), row(c_ln_b[slot]),
                              c_w_s[slot], c_b_s[slot][:, :, None], c_w_out.astype(BF16), g1, b1,
                              rwt, rb)
        x1, x1lin, eidx, wcol = outs
        y = _moe(x1lin, eidx, wcol, i, moe_wg, moe_wu, moe_wd)
        xt = _post(x1, y, i, pt, row(ln_g[i, 1]), row(ln_b[i, 1]), ple_wp, ple_wg,
                   row(ple_b_gate[i]))
    return xt.reshape(bsz, seq, d)
```

```python
import functools
import math

import numpy as np
import jax
import jax.numpy as jnp
from jax import lax
from jax.experimental import pallas as pl
from jax.experimental.pallas import tpu as pltpu

F32 = jnp.float32
BF16 = jnp.bfloat16

D_MODEL = 1024
DEPTH = 4
N_MIXERS = 3
CONV_WIDTH = 3
RET_HEADS = 4
RET_DK = D_MODEL // RET_HEADS
RET_DV = 2 * RET_DK
RET_VDIM = RET_HEADS * RET_DV
RET_CHUNK = 128
ROPE_BASE = 10000.0
SGU_HALF = 3 * D_MODEL
SGU_GROUPS = 4
SGU_GD = SGU_HALF // SGU_GROUPS
SGU_CHUNK = 128
N_EXPERTS = 16
N_GROUPS = 4
EXPERTS_PER_GROUP = N_EXPERTS // N_GROUPS
D_EXPERT = D_MODEL // 2
PLE_DIM = 256
ALPHA = (2 * DEPTH) ** 0.25
LN_EPS = 1e-5

VMEM_LIMIT_BYTES = 56 * 1024 * 1024
SUBLANES = 8
LANES = 128
LANE_TILES = D_MODEL // LANES

TM_CONV = 512
SUB_CONV = 512
TM_RET = 256
TM_SGU = 256
TM_POST = 512
TB_MOE = 2048
CH_MOE = 320
CHX_MOE = 64
ROW_UNROLL = 16


def _const_spec(shape):
    nd = len(shape)
    return pl.BlockSpec(shape, lambda *_: (0,) * nd, pipeline_mode=pl.Buffered(1))


def _slot_spec(shape, slot):
    nd = len(shape)
    return pl.BlockSpec((1,) + tuple(shape[1:]), lambda *_: (slot,) + (0,) * (nd - 1),
                        pipeline_mode=pl.Buffered(1))


def _layer_norm(h, g, b):
    mu = jnp.mean(h, axis=-1, keepdims=True)
    hc = h - mu
    var = jnp.mean(hc * hc, axis=-1, keepdims=True)
    return hc * lax.rsqrt(var + LN_EPS) * g + b


def _top2_sum(a, b, c, d):
    m1, n1 = jnp.maximum(a, b), jnp.minimum(a, b)
    m2, n2 = jnp.maximum(c, d), jnp.minimum(c, d)
    return jnp.maximum(m1, m2) + jnp.maximum(jnp.minimum(m1, m2), jnp.maximum(n1, n2))


def _route(x1b, rwt_ref, rb_ref):
    tm = x1b.shape[0]
    logits_t = lax.dot_general(rwt_ref[...], x1b, (((1,), (1,)), ((), ())),
                               preferred_element_type=F32)
    scores = jax.nn.sigmoid(logits_t)
    sel = scores + rb_ref[...]
    s = [sel[e:e + 1, :] for e in range(N_EXPERTS)]
    sc = [scores[e:e + 1, :] for e in range(N_EXPERTS)]
    grp = [_top2_sum(*s[4 * g:4 * g + 4]) for g in range(N_GROUPS)]
    best, g_idx = grp[0], jnp.zeros((1, tm), jnp.int32)
    for g in range(1, N_GROUPS):
        better = grp[g] > best
        best = jnp.where(better, grp[g], best)
        g_idx = jnp.where(better, g, g_idx)

    def pick(rows, j):
        out = rows[j]
        for g in range(1, N_GROUPS):
            out = jnp.where(g_idx == g, rows[4 * g + j], out)
        return out
    v = [pick(s, j) for j in range(EXPERTS_PER_GROUP)]
    w = [pick(sc, j) for j in range(EXPERTS_PER_GROUP)]

    def argmax4(vals):
        bv, bi = vals[0], jnp.zeros((1, tm), jnp.int32)
        for j in range(1, EXPERTS_PER_GROUP):
            better = vals[j] > bv
            bv = jnp.where(better, vals[j], bv)
            bi = jnp.where(better, j, bi)
        return bi
    i1 = argmax4(v)
    i2 = argmax4([jnp.where(i1 == j, -jnp.inf, v[j]) for j in range(EXPERTS_PER_GROUP)])

    def take(vals, idx):
        out = vals[0]
        for j in range(1, EXPERTS_PER_GROUP):
            out = jnp.where(idx == j, vals[j], out)
        return out
    w1, w2 = take(w, i1), take(w, i2)
    den = w1 + w2
    w1, w2 = w1 / den, w2 / den
    e1, e2 = g_idx * EXPERTS_PER_GROUP + i1, g_idx * EXPERTS_PER_GROUP + i2
    return e1, e2, w1, w2


def _finish_mixer(x, mix, r0, lng_ref, lnb_ref, rwt_ref, rb_ref, x1lin_ref, eidx_ref,
                  wcol_ref):
    tm = x.shape[0]
    x1 = _layer_norm(ALPHA * x + mix, lng_ref[...], lnb_ref[...])
    for c in range(LANE_TILES):
        x1lin_ref[pl.ds(r0 * LANE_TILES + c, tm, stride=LANE_TILES), :] = (
            x1[:, c * LANES:(c + 1) * LANES])
    e1, e2, w1, w2 = _route(x1.astype(BF16), rwt_ref, rb_ref)
    eidx_ref[:, r0:r0 + tm] = jnp.concatenate([e1, e2], axis=0)
    pad = jnp.zeros((LANES - 2, tm), F32)
    wcol_ref[r0:r0 + tm, :] = jnp.concatenate([w1, w2, pad], axis=0).T


def _conv_mixer_kernel(tiles_per_seq, x_ref, win_ref, cw_ref, wout_ref, lng_ref, lnb_ref,
                       rwt_ref, rb_ref, x1lin_ref, eidx_ref, wcol_ref, carry_ref):
    i = pl.program_id(0)
    ts = SUB_CONV

    @pl.when(i % tiles_per_seq == 0)
    def _():
        carry_ref[...] = jnp.zeros_like(carry_ref)
    prev = carry_ref[...]
    row = lax.broadcasted_iota(jnp.int32, (ts, 1), 0)
    cw = cw_ref[...]
    for r0 in range(0, x_ref.shape[0], ts):
        x = x_ref[r0:r0 + ts, :]
        bch = jnp.dot(x.astype(BF16), win_ref[0], preferred_element_type=F32)
        b_gate = bch[:, :D_MODEL]
        z = bch[:, D_MODEL:2 * D_MODEL] * bch[:, 2 * D_MODEL:]
        p1, p2 = prev[SUBLANES - 1:SUBLANES], prev[SUBLANES - 2:SUBLANES - 1]
        z1 = jnp.where(row == 0, p1, pltpu.roll(z, 1, axis=0))
        z2 = jnp.where(row == 0, p2, jnp.where(row == 1, p1, pltpu.roll(z, 2, axis=0)))
        prev = z[ts - SUBLANES:, :]
        zc = cw[0:1] * z2 + cw[1:2] * z1 + cw[2:3] * z
        mix = jnp.dot((b_gate * zc).astype(BF16), wout_ref[0], preferred_element_type=F32)
        _finish_mixer(x, mix, r0, lng_ref, lnb_ref, rwt_ref, rb_ref, x1lin_ref, eidx_ref,
                      wcol_ref)
    carry_ref[...] = prev


def _mixer_out_shapes(t):
    return (jax.ShapeDtypeStruct((t * LANE_TILES, LANES), F32),
            jax.ShapeDtypeStruct((2, t), jnp.int32),
            jax.ShapeDtypeStruct((t, LANES), F32))


def _mixer_out_specs(tm):
    return (pl.BlockSpec((tm * LANE_TILES, LANES), lambda i: (i, 0)),
            pl.BlockSpec((2, tm), lambda i: (0, i)),
            pl.BlockSpec((tm, LANES), lambda i: (i, 0)))


def _conv_mixer(x, seq, slot, w_in, conv_w, w_out, ln_g, ln_b, rwt, rb):
    t = x.shape[0]
    tm = TM_CONV
    return pl.pallas_call(
        functools.partial(_conv_mixer_kernel, seq // tm),
        out_shape=_mixer_out_shapes(t),
        grid=(t // tm,),
        in_specs=[pl.BlockSpec((tm, D_MODEL), lambda i: (i, 0)),
                  _slot_spec(w_in.shape, slot), _const_spec(conv_w.shape),
                  _slot_spec(w_out.shape, slot),
                  _const_spec(ln_g.shape), _const_spec(ln_b.shape),
                  _const_spec(rwt.shape), _const_spec(rb.shape)],
        out_specs=_mixer_out_specs(tm),
        scratch_shapes=[pltpu.VMEM((SUBLANES, D_MODEL), F32)],
        compiler_params=pltpu.CompilerParams(dimension_semantics=("arbitrary",),
                                             vmem_limit_bytes=VMEM_LIMIT_BYTES),
        name="conv_mixer",
    )(x, w_in, conv_w, w_out, ln_g, ln_b, rwt, rb)


def _rotary(t, cos, sin):
    half = t.shape[-1] // 2
    t1, t2 = t[:, :half], t[:, half:]
    return jnp.concatenate([t1 * cos - t2 * sin, t2 * cos + t1 * sin], axis=-1)


def _retention_kernel(chunks_per_seq, x_ref, win_ref, wout_ref, cos_ref, sin_ref, dmask_ref,
                      qdec_ref, kdec_ref, cdec_ref, lng_ref, lnb_ref, rwt_ref, rb_ref,
                      x1lin_ref, eidx_ref, wcol_ref, state_ref):
    i = pl.program_id(0)

    @pl.when(i % chunks_per_seq == 0)
    def _():
        state_ref[...] = jnp.zeros_like(state_ref)

    for r0 in range(0, x_ref.shape[0], RET_CHUNK):
        _retention_chunk(r0, x_ref, win_ref, wout_ref, cos_ref, sin_ref, dmask_ref, qdec_ref,
                         kdec_ref, cdec_ref, lng_ref, lnb_ref, rwt_ref, rb_ref, x1lin_ref,
                         eidx_ref, wcol_ref, state_ref)


def _retention_chunk(r0, x_ref, win_ref, wout_ref, cos_ref, sin_ref, dmask_ref, qdec_ref, kdec_ref,
                     cdec_ref, lng_ref, lnb_ref, rwt_ref, rb_ref, x1lin_ref, eidx_ref,
                     wcol_ref, state_ref):
    rows = slice(r0, r0 + RET_CHUNK)
    x = x_ref[rows, :]
    proj = jnp.dot(x.astype(BF16), win_ref[0], preferred_element_type=F32)
    cos, sin = cos_ref[rows, :], sin_ref[rows, :]
    gated = []
    for h in range(RET_HEADS):
        q = _rotary(proj[:, h * RET_DK:(h + 1) * RET_DK], cos, sin)
        k = _rotary(proj[:, D_MODEL + h * RET_DK:D_MODEL + (h + 1) * RET_DK] * (RET_DK ** -0.5),
                    cos, sin)
        v = proj[:, 2 * D_MODEL + h * RET_DV:2 * D_MODEL + (h + 1) * RET_DV]
        g = proj[:, 2 * D_MODEL + RET_VDIM + h * RET_DV:2 * D_MODEL + RET_VDIM + (h + 1) * RET_DV]
        qb, vb = q.astype(BF16), v.astype(BF16)
        scores = lax.dot_general(qb, k.astype(BF16), (((1,), (1,)), ((), ())),
                                 preferred_element_type=F32) * dmask_ref[h]
        inner = jnp.dot(scores.astype(BF16), vb, preferred_element_type=F32)
        state = state_ref[h]
        cross = jnp.dot(qb, state.astype(BF16), preferred_element_type=F32) * qdec_ref[h]
        kd = (k * kdec_ref[h]).astype(BF16)
        state_ref[h] = state * cdec_ref[h] + lax.dot_general(
            kd, vb, (((0,), (0,)), ((), ())), preferred_element_type=F32)
        o = inner + cross
        mu = jnp.mean(o, axis=-1, keepdims=True)
        oc = o - mu
        o = oc * lax.rsqrt(jnp.mean(oc * oc, axis=-1, keepdims=True) + LN_EPS)
        gated.append((jax.nn.silu(g) * o).astype(BF16))
    mix = jnp.dot(jnp.concatenate(gated, axis=-1), wout_ref[0], preferred_element_type=F32)
    _finish_mixer(x, mix, r0, lng_ref, lnb_ref, rwt_ref, rb_ref, x1lin_ref, eidx_ref,
                  wcol_ref)


def _retention_tables(seq):
    c = RET_CHUNK
    pos = jnp.arange(seq, dtype=F32)
    inv_freq = 1.0 / (ROPE_BASE ** jnp.linspace(0.0, 1.0, RET_DK // 2, dtype=F32))
    ang = pos[:, None] * inv_freq[None, :]
    log_gamma = jnp.log(1.0 - 2.0 ** (-5.0 - jnp.arange(RET_HEADS, dtype=F32)))
    idx = jnp.arange(c, dtype=F32)
    diff = idx[:, None] - idx[None, :]
    dmask = jnp.where(diff >= 0, jnp.exp(log_gamma[:, None, None] * jnp.maximum(diff, 0.0)), 0.0)
    qdec = jnp.exp(log_gamma[:, None] * (idx[None, :] + 1.0))[:, :, None]
    kdec = jnp.exp(log_gamma[:, None] * (c - 1.0 - idx[None, :]))[:, :, None]
    cdec = jnp.broadcast_to(jnp.exp(log_gamma * c)[:, None, None], (RET_HEADS, 1, RET_DV))
    return jnp.cos(ang), jnp.sin(ang), dmask, qdec, kdec, cdec


def _retention_mixer(x, seq, slot, w_in, w_out, ln_g, ln_b, rwt, rb):
    t = x.shape[0]
    c = TM_RET
    cps = seq // c
    cos, sin, dmask, qdec, kdec, cdec = _retention_tables(seq)
    return pl.pallas_call(
        functools.partial(_retention_kernel, cps),
        out_shape=_mixer_out_shapes(t),
        grid=(t // c,),
        in_specs=[pl.BlockSpec((c, D_MODEL), lambda i: (i, 0)),
                  _slot_spec(w_in.shape, slot), _slot_spec(w_out.shape, slot),
                  pl.BlockSpec((c, RET_DK // 2), lambda i: (i % cps, 0)),
                  pl.BlockSpec((c, RET_DK // 2), lambda i: (i % cps, 0)),
                  _const_spec(dmask.shape), _const_spec(qdec.shape), _const_spec(kdec.shape),
                  _const_spec(cdec.shape),
                  _const_spec(ln_g.shape), _const_spec(ln_b.shape),
                  _const_spec(rwt.shape), _const_spec(rb.shape)],
        out_specs=_mixer_out_specs(c),
        scratch_shapes=[pltpu.VMEM((RET_HEADS, RET_DK, RET_DV), F32)],
        compiler_params=pltpu.CompilerParams(dimension_semantics=("arbitrary",),
                                             vmem_limit_bytes=VMEM_LIMIT_BYTES),
        name="retention_mixer",
    )(x, w_in, w_out, cos, sin, dmask, qdec, kdec, cdec, ln_g, ln_b, rwt, rb)


def _sgu_kernel(x_ref, win_ref, sg_ref, sb_ref, ws_ref, bs_ref, wout_ref, lng_ref, lnb_ref,
                rwt_ref, rb_ref, x1lin_ref, eidx_ref, wcol_ref):
    c = SGU_CHUNK
    row = lax.broadcasted_iota(jnp.int32, (c, c), 0)
    col = lax.broadcasted_iota(jnp.int32, (c, c), 1)
    ws = [jnp.where(row >= col, ws_ref[g], 0.0).astype(BF16) for g in range(SGU_GROUPS)]
    for r0 in range(0, x_ref.shape[0], c):
        x = x_ref[r0:r0 + c, :]
        zin = jnp.dot(x.astype(BF16), win_ref[0], preferred_element_type=F32)
        z = 0.5 * zin * (1.0 + lax.erf(zin * (2.0 ** -0.5)))
        u = z[:, :SGU_HALF]
        v = _layer_norm(z[:, SGU_HALF:], sg_ref[...], sb_ref[...]).astype(BF16)
        parts = []
        for g in range(SGU_GROUPS):
            vs = jnp.dot(ws[g], v[:, g * SGU_GD:(g + 1) * SGU_GD], preferred_element_type=F32)
            parts.append((u[:, g * SGU_GD:(g + 1) * SGU_GD] * (vs + bs_ref[g])).astype(BF16))
        mix = jnp.dot(jnp.concatenate(parts, axis=-1), wout_ref[0], preferred_element_type=F32)
        _finish_mixer(x, mix, r0, lng_ref, lnb_ref, rwt_ref, rb_ref, x1lin_ref, eidx_ref,
                      wcol_ref)


def _sgu_mixer(x, slot, w_in, sgu_g, sgu_b, w_s, b_s, w_out, ln_g, ln_b, rwt, rb):
    t = x.shape[0]
    c = TM_SGU
    return pl.pallas_call(
        _sgu_kernel,
        out_shape=_mixer_out_shapes(t),
        grid=(t // c,),
        in_specs=[pl.BlockSpec((c, D_MODEL), lambda i: (i, 0)),
                  _slot_spec(w_in.shape, slot), _const_spec(sgu_g.shape), _const_spec(sgu_b.shape),
                  _const_spec(w_s.shape), _const_spec(b_s.shape), _slot_spec(w_out.shape, slot),
                  _const_spec(ln_g.shape), _const_spec(ln_b.shape),
                  _const_spec(rwt.shape), _const_spec(rb.shape)],
        out_specs=_mixer_out_specs(c),
        compiler_params=pltpu.CompilerParams(dimension_semantics=("arbitrary",),
                                             vmem_limit_bytes=VMEM_LIMIT_BYTES),
        name="sgu_mixer",
    )(x, w_in, sgu_g, sgu_b, w_s, b_s, w_out, ln_g, ln_b, rwt, rb)


PLAN_BL = 256


def _plan_kernel(eidx_ref, dest_ref, offs_ref):
    tb = eidx_ref.shape[1]
    ns = 2 * tb
    e_all = jnp.concatenate([eidx_ref[0:1, :], eidx_ref[1:2, :]], axis=1)
    eid = lax.broadcasted_iota(jnp.int32, (N_EXPERTS, ns), 0)
    mask = eid == e_all
    off = jnp.sum(jnp.where(e_all < eid, 1.0, 0.0), axis=1, keepdims=True)
    r = lax.broadcasted_iota(jnp.int32, (PLAN_BL, PLAN_BL), 0)
    c = lax.broadcasted_iota(jnp.int32, (PLAN_BL, PLAN_BL), 1)
    upper = jnp.where(r < c, 1.0, 0.0).astype(BF16)
    carry = off
    pieces = []
    for b in range(ns // PLAN_BL):
        mb = mask[:, b * PLAN_BL:(b + 1) * PLAN_BL]
        mbf = jnp.where(mb, 1.0, 0.0)
        pos = jnp.dot(mbf.astype(BF16), upper, preferred_element_type=F32) + carry
        pieces.append(jnp.sum(jnp.where(mb, pos, 0.0), axis=0, keepdims=True))
        carry = carry + jnp.sum(mbf, axis=1, keepdims=True)
    dest_ref[0] = jnp.concatenate(pieces, axis=1).astype(jnp.int32)
    sub = lax.broadcasted_iota(jnp.int32, (N_EXPERTS, LANES), 0)
    lane = lax.broadcasted_iota(jnp.int32, (N_EXPERTS, LANES), 1)
    offs_row = jnp.sum(jnp.where(sub == lane, off, 0.0), axis=0, keepdims=True)
    lane1 = lax.broadcasted_iota(jnp.int32, (1, LANES), 1)
    offs_ref[0] = jnp.where(lane1 == N_EXPERTS, float(ns), offs_row).astype(jnp.int32)


def _plan(eidx, tb):
    t = eidx.shape[1]
    nb = t // tb
    return pl.pallas_call(
        _plan_kernel,
        out_shape=(jax.ShapeDtypeStruct((nb, 1, 2 * tb), jnp.int32),
                   jax.ShapeDtypeStruct((nb, 1, LANES), jnp.int32)),
        grid=(nb,),
        in_specs=[pl.BlockSpec((2, tb), lambda i: (0, i))],
        out_specs=(pl.BlockSpec((1, 1, 2 * tb), lambda i: (i, 0, 0)),
                   pl.BlockSpec((1, 1, LANES), lambda i: (i, 0, 0))),
        compiler_params=pltpu.CompilerParams(dimension_semantics=("arbitrary",)),
        name="moe_plan",
    )(eidx)


def _expert_ffn(g_ref, o_ref, wg_ref, wu_ref, wd_ref):
    ch = g_ref.shape[0] // LANE_TILES
    xg = jnp.concatenate([g_ref[pl.ds(cc, ch, stride=LANE_TILES), :]
                          for cc in range(LANE_TILES)], axis=1).astype(BF16)
    hg = jnp.dot(xg, wg_ref[0], preferred_element_type=F32)
    hu = jnp.dot(xg, wu_ref[0], preferred_element_type=F32)
    he = (jax.nn.silu(hg) * hu).astype(BF16)
    o = jnp.dot(he, wd_ref[0], preferred_element_type=F32)
    for cc in range(LANE_TILES):
        o_ref[pl.ds(cc, ch, stride=LANE_TILES), :] = o[:, cc * LANES:(cc + 1) * LANES]


def _moe_kernel(dest_ref, offs_ref, z_ref, wcol_ref, wg_ref, wu_ref, wd_ref, y_ref,
                inv_ref, g_ref, o_ref, gx_ref, ox_ref, y2_ref):
    i = pl.program_id(0)
    e = pl.program_id(1)
    tb = y_ref.shape[0]
    ns = 2 * tb
    ch = g_ref.shape[1] // LANE_TILES
    chx = gx_ref.shape[0] // LANE_TILES
    tok_mask = tb * LANE_TILES - 1
    rows_at = lambda off: pl.ds(pl.multiple_of(off, LANE_TILES), LANE_TILES)
    cur = e % 2
    nxt = 1 - cur

    @pl.when((i == 0) & (e == 0))
    def _():
        o_ref[...] = jnp.zeros_like(o_ref)

    @pl.when(e == 0)
    def _():
        def pad_body(k, carry):
            inv_ref[ns + k] = ns * LANE_TILES
            return carry
        lax.fori_loop(0, ch, pad_body, 0)

        def inv_body(k, carry):
            for u in range(ROW_UNROLL):
                s = k * ROW_UNROLL + u
                inv_ref[dest_ref[0, 0, s]] = s * LANE_TILES
            return carry
        lax.fori_loop(0, ns // ROW_UNROLL, inv_body, 0)

        def gather_body(k, carry):
            for u in range(ROW_UNROLL):
                j = k * ROW_UNROLL + u
                g_ref[0, rows_at(j * LANE_TILES), :] = z_ref[rows_at(inv_ref[j] & tok_mask), :]
            return carry
        lax.fori_loop(0, ch // ROW_UNROLL, gather_body, 0)

    start = offs_ref[0, 0, e]
    end = offs_ref[0, 0, e + 1]
    prev_start = jnp.where(e == 0, ns, offs_ref[0, 0, jnp.maximum(e - 1, 0)])

    for j in range(ch):
        y2_ref[rows_at(inv_ref[prev_start + j]), :] = o_ref[nxt, j * LANE_TILES:(j + 1) * LANE_TILES, :]
    _expert_ffn(g_ref.at[cur], o_ref.at[cur], wg_ref, wu_ref, wd_ref)
    for j in range(ch):
        g_ref[nxt, j * LANE_TILES:(j + 1) * LANE_TILES, :] = (
            z_ref[rows_at(inv_ref[end + j] & tok_mask), :])

    def extra_body(c, carry):
        a = start + ch + c * chx

        def gather_body(k, carry2):
            for u in range(SUBLANES):
                j = k * SUBLANES + u
                gx_ref[rows_at(j * LANE_TILES), :] = z_ref[rows_at(inv_ref[a + j] & tok_mask), :]
            return carry2
        lax.fori_loop(0, chx // SUBLANES, gather_body, 0)
        _expert_ffn(gx_ref, ox_ref, wg_ref, wu_ref, wd_ref)

        def scatter_body(k, carry2):
            for u in range(SUBLANES):
                j = k * SUBLANES + u
                y2_ref[rows_at(inv_ref[a + j]), :] = ox_ref[rows_at(j * LANE_TILES), :]
            return carry2
        lax.fori_loop(0, chx // SUBLANES, scatter_body, 0)
        return carry
    lax.fori_loop(0, (jnp.maximum(end - start - ch, 0) + (chx - 1)) // chx, extra_body, 0)

    @pl.when(e == N_EXPERTS - 1)
    def _():
        def scatter_body(k, carry):
            for u in range(ROW_UNROLL):
                j = k * ROW_UNROLL + u
                y2_ref[rows_at(inv_ref[start + j]), :] = o_ref[cur, rows_at(j * LANE_TILES), :]
            return carry
        lax.fori_loop(0, ch // ROW_UNROLL, scatter_body, 0)
        w = wcol_ref[...]
        w1 = jnp.broadcast_to(w[:, 0:1], (tb, LANES))
        w2 = jnp.broadcast_to(w[:, 1:2], (tb, LANES))
        for cc in range(LANE_TILES):
            y_first = y2_ref[pl.ds(cc, tb, stride=LANE_TILES), :]
            y_second = y2_ref[pl.ds(tb * LANE_TILES + cc, tb, stride=LANE_TILES), :]
            y_ref[:, cc * LANES:(cc + 1) * LANES] = w1 * y_first + w2 * y_second


def _moe(x1lin, eidx, wcol, layer, w_gate, w_up, w_down):
    t = wcol.shape[0]
    tb, ch = TB_MOE, CH_MOE
    dest, offs = _plan(eidx, tb)
    smem_spec = lambda n: pl.BlockSpec((1, 1, n), lambda i, e: (i, 0, 0), memory_space=pltpu.SMEM)
    expert_block = lambda i, e: (layer * N_EXPERTS + e, 0, 0)
    return pl.pallas_call(
        _moe_kernel,
        out_shape=jax.ShapeDtypeStruct((t, D_MODEL), F32),
        grid=(t // tb, N_EXPERTS),
        in_specs=[smem_spec(2 * tb), smem_spec(LANES),
                  pl.BlockSpec((tb * LANE_TILES, LANES), lambda i, e: (i, 0),
                               pipeline_mode=pl.Buffered(1)),
                  pl.BlockSpec((tb, LANES), lambda i, e: (i, 0)),
                  pl.BlockSpec((1, D_MODEL, D_EXPERT), expert_block),
                  pl.BlockSpec((1, D_MODEL, D_EXPERT), expert_block),
                  pl.BlockSpec((1, D_EXPERT, D_MODEL), expert_block)],
        out_specs=pl.BlockSpec((tb, D_MODEL), lambda i, e: (i, 0), pipeline_mode=pl.Buffered(1)),
        scratch_shapes=[pltpu.SMEM((2 * tb + ch,), jnp.int32),
                        pltpu.VMEM((2, ch * LANE_TILES, LANES), F32),
                        pltpu.VMEM((2, ch * LANE_TILES, LANES), F32),
                        pltpu.VMEM((CHX_MOE * LANE_TILES, LANES), F32),
                        pltpu.VMEM((CHX_MOE * LANE_TILES, LANES), F32),
                        pltpu.VMEM(((2 * tb + 1) * LANE_TILES, LANES), F32)],
        compiler_params=pltpu.CompilerParams(dimension_semantics=("arbitrary", "arbitrary"),
                                             vmem_limit_bytes=VMEM_LIMIT_BYTES),
        name="moe",
    )(dest, offs, x1lin, wcol, w_gate, w_up, w_down)


def _post_kernel(x1lin_ref, y_ref, p_ref, lng_ref, lnb_ref, wp_ref, wg_ref, bg_ref, o_ref):
    tm = y_ref.shape[0]
    x1 = jnp.concatenate([x1lin_ref[pl.ds(c, tm, stride=LANE_TILES), :]
                          for c in range(LANE_TILES)], axis=1)
    x2 = _layer_norm(ALPHA * x1 + y_ref[...], lng_ref[...], lnb_ref[...])
    emb = jnp.dot(p_ref[0].astype(BF16), wp_ref[0], preferred_element_type=F32)
    gate = jax.nn.sigmoid(
        jnp.dot(x2.astype(BF16), wg_ref[0], preferred_element_type=F32) + bg_ref[...])
    o_ref[...] = x2 + gate * emb


def _post(x1lin, y, layer, p, ln_g, ln_b, w_proj, w_gate, b_gate):
    t = y.shape[0]
    tm = TM_POST
    return pl.pallas_call(
        _post_kernel,
        out_shape=jax.ShapeDtypeStruct((t, D_MODEL), F32),
        grid=(t // tm,),
        in_specs=[pl.BlockSpec((tm * LANE_TILES, LANES), lambda i: (i, 0)),
                  pl.BlockSpec((tm, D_MODEL), lambda i: (i, 0)),
                  pl.BlockSpec((1, tm, PLE_DIM), lambda i: (layer, i, 0)),
                  _const_spec(ln_g.shape), _const_spec(ln_b.shape),
                  _slot_spec(w_proj.shape, layer), _slot_spec(w_gate.shape, layer),
                  _const_spec(b_gate.shape)],
        out_specs=pl.BlockSpec((tm, D_MODEL), lambda i: (i, 0)),
        compiler_params=pltpu.CompilerParams(dimension_semantics=("arbitrary",),
                                             vmem_limit_bytes=VMEM_LIMIT_BYTES),
        name="post_moe",
    )(x1lin, y, p, ln_g, ln_b, w_proj, w_gate, b_gate)


def kernel(x, p, a_w_in, a_conv_w, a_w_out, b_w_in, b_w_out, c_w_in, c_ln_g, c_ln_b, c_w_s, c_b_s,
           c_w_out, router_w, router_b, moe_w_gate, moe_w_up, moe_w_down, ln_g, ln_b, ple_w_proj,
           ple_w_gate, ple_b_gate):
    bsz, seq, d = x.shape
    t = bsz * seq
    xt = x.reshape(t, d)
    pt = p.reshape(DEPTH, t, PLE_DIM)
    rwt = router_w.T.astype(BF16)
    rb = router_b.astype(F32).reshape(N_EXPERTS, 1)
    row = lambda a: a.reshape(1, -1)
    a_w_in_b, a_w_out_b = a_w_in.astype(BF16), a_w_out.astype(BF16)
    moe_shape = lambda w: w.astype(BF16).reshape((DEPTH * N_EXPERTS,) + w.shape[2:])
    moe_wg, moe_wu, moe_wd = moe_shape(moe_w_gate), moe_shape(moe_w_up), moe_shape(moe_w_down)
    ple_wp, ple_wg = ple_w_proj.astype(BF16), ple_w_gate.astype(BF16)
    for i in range(DEPTH):
        mixer, slot = i % N_MIXERS, i // N_MIXERS
        g1, b1 = row(ln_g[i, 0]), row(ln_b[i, 0])
        if mixer == 0:
            outs = _conv_mixer(xt, seq, slot, a_w_in_b, a_conv_w[slot], a_w_out_b, g1, b1, rwt, rb)
        elif mixer == 1:
            outs = _retention_mixer(xt, seq, slot, b_w_in.astype(BF16), b_w_out.astype(BF16), g1,
                                    b1, rwt, rb)
        else:
            outs = _sgu_mixer(xt, slot, c_w_in.astype(BF16), row(c_ln_g[slot]), row(c_ln_b[slot]),
                              c_w_s[slot], c_b_s[slot][:, :, None], c_w_out.astype(BF16), g1, b1,
                              rwt, rb)
        x1lin, eidx, wcol = outs
        y = _moe(x1lin, eidx, wcol, i, moe_wg, moe_wu, moe_wd)
        xt = _post(x1lin, y, i, pt, row(ln_g[i, 1]), row(ln_b[i, 1]), ple_wp, ple_wg,
                   row(ple_b_gate[i]))
    return xt.reshape(bsz, seq, d)
```

```python
import functools
import math

import numpy as np
import jax
import jax.numpy as jnp
from jax import lax
from jax.experimental import pallas as pl
from jax.experimental.pallas import tpu as pltpu

F32 = jnp.float32
BF16 = jnp.bfloat16

D_MODEL = 1024
DEPTH = 4
N_MIXERS = 3
CONV_WIDTH = 3
RET_HEADS = 4
RET_DK = D_MODEL // RET_HEADS
RET_DV = 2 * RET_DK
RET_VDIM = RET_HEADS * RET_DV
RET_CHUNK = 128
ROPE_BASE = 10000.0
SGU_HALF = 3 * D_MODEL
SGU_GROUPS = 4
SGU_GD = SGU_HALF // SGU_GROUPS
SGU_CHUNK = 128
N_EXPERTS = 16
N_GROUPS = 4
EXPERTS_PER_GROUP = N_EXPERTS // N_GROUPS
D_EXPERT = D_MODEL // 2
PLE_DIM = 256
ALPHA = (2 * DEPTH) ** 0.25
LN_EPS = 1e-5

VMEM_LIMIT_BYTES = 56 * 1024 * 1024
SUBLANES = 8
LANES = 128
LANE_TILES = D_MODEL // LANES

TM_CONV = 1024
CONV_COLS = 1024
TM_RET = 512
TM_SGU = 512
TM_POST = 1024
TB_MOE = 2048
CH_MOE = 320
CHX_MOE = 64
ROW_UNROLL = 16


def _const_spec(shape):
    nd = len(shape)
    return pl.BlockSpec(shape, lambda *_: (0,) * nd, pipeline_mode=pl.Buffered(1))


def _slot_spec(shape, slot):
    nd = len(shape)
    return pl.BlockSpec((1,) + tuple(shape[1:]), lambda *_: (slot,) + (0,) * (nd - 1),
                        pipeline_mode=pl.Buffered(1))


def _layer_norm(h, g, b):
    mu = jnp.mean(h, axis=-1, keepdims=True)
    hc = h - mu
    var = jnp.mean(hc * hc, axis=-1, keepdims=True)
    return hc * lax.rsqrt(var + LN_EPS) * g + b


def _top2_sum(a, b, c, d):
    m1, n1 = jnp.maximum(a, b), jnp.minimum(a, b)
    m2, n2 = jnp.maximum(c, d), jnp.minimum(c, d)
    return jnp.maximum(m1, m2) + jnp.maximum(jnp.minimum(m1, m2), jnp.maximum(n1, n2))


def _route(x1b, rwt_ref, rb_ref):
    tm = x1b.shape[0]
    logits_t = lax.dot_general(rwt_ref[...], x1b, (((1,), (1,)), ((), ())),
                               preferred_element_type=F32)
    scores = jax.nn.sigmoid(logits_t)
    sel = scores + rb_ref[...]
    s = [sel[e:e + 1, :] for e in range(N_EXPERTS)]
    sc = [scores[e:e + 1, :] for e in range(N_EXPERTS)]
    grp = [_top2_sum(*s[4 * g:4 * g + 4]) for g in range(N_GROUPS)]
    best, g_idx = grp[0], jnp.zeros((1, tm), jnp.int32)
    for g in range(1, N_GROUPS):
        better = grp[g] > best
        best = jnp.where(better, grp[g], best)
        g_idx = jnp.where(better, g, g_idx)

    def pick(rows, j):
        out = rows[j]
        for g in range(1, N_GROUPS):
            out = jnp.where(g_idx == g, rows[4 * g + j], out)
        return out
    v = [pick(s, j) for j in range(EXPERTS_PER_GROUP)]
    w = [pick(sc, j) for j in range(EXPERTS_PER_GROUP)]

    def argmax4(vals):
        bv, bi = vals[0], jnp.zeros((1, tm), jnp.int32)
        for j in range(1, EXPERTS_PER_GROUP):
            better = vals[j] > bv
            bv = jnp.where(better, vals[j], bv)
            bi = jnp.where(better, j, bi)
        return bi
    i1 = argmax4(v)
    i2 = argmax4([jnp.where(i1 == j, -jnp.inf, v[j]) for j in range(EXPERTS_PER_GROUP)])

    def take(vals, idx):
        out = vals[0]
        for j in range(1, EXPERTS_PER_GROUP):
            out = jnp.where(idx == j, vals[j], out)
        return out
    w1, w2 = take(w, i1), take(w, i2)
    den = w1 + w2
    w1, w2 = w1 / den, w2 / den
    e1, e2 = g_idx * EXPERTS_PER_GROUP + i1, g_idx * EXPERTS_PER_GROUP + i2
    return e1, e2, w1, w2


def _finish_mixer(x, mix, r0, lng_ref, lnb_ref, rwt_ref, rb_ref, x1lin_ref, eidx_ref,
                  wcol_ref):
    tm = x.shape[0]
    x1 = _layer_norm(ALPHA * x + mix, lng_ref[...], lnb_ref[...])
    for c in range(LANE_TILES):
        x1lin_ref[pl.ds(r0 * LANE_TILES + c, tm, stride=LANE_TILES), :] = (
            x1[:, c * LANES:(c + 1) * LANES])
    e1, e2, w1, w2 = _route(x1.astype(BF16), rwt_ref, rb_ref)
    eidx_ref[:, r0:r0 + tm] = jnp.concatenate([e1, e2], axis=0)
    pad = jnp.zeros((LANES - 2, tm), F32)
    wcol_ref[r0:r0 + tm, :] = jnp.concatenate([w1, w2, pad], axis=0).T


def _conv_mixer_kernel(tiles_per_seq, x_ref, win_ref, cw_ref, wout_ref, lng_ref, lnb_ref,
                       rwt_ref, rb_ref, x1lin_ref, eidx_ref, wcol_ref, carry_ref):
    i = pl.program_id(0)
    tm = x_ref.shape[0]
    cg = CONV_COLS

    @pl.when(i % tiles_per_seq == 0)
    def _():
        carry_ref[...] = jnp.zeros_like(carry_ref)
    row = lax.broadcasted_iota(jnp.int32, (tm, 1), 0)
    x = x_ref[...]
    xb = x.astype(BF16)

    def proj(lo):
        return jnp.dot(xb, win_ref[0, :, lo:lo + cg], preferred_element_type=F32)

    mix = None
    for lo in range(0, D_MODEL, cg):
        b_gate = proj(lo)
        z = proj(D_MODEL + lo) * proj(2 * D_MODEL + lo)
        prev = carry_ref[:, lo:lo + cg]
        p1, p2 = prev[SUBLANES - 1:SUBLANES], prev[SUBLANES - 2:SUBLANES - 1]
        z1 = jnp.where(row == 0, p1, pltpu.roll(z, 1, axis=0))
        z2 = jnp.where(row == 0, p2, jnp.where(row == 1, p1, pltpu.roll(z, 2, axis=0)))
        carry_ref[:, lo:lo + cg] = z[tm - SUBLANES:, :]
        cw = cw_ref[:, lo:lo + cg]
        zc = cw[0:1] * z2 + cw[1:2] * z1 + cw[2:3] * z
        part = jnp.dot((b_gate * zc).astype(BF16), wout_ref[0, lo:lo + cg, :],
                       preferred_element_type=F32)
        mix = part if mix is None else mix + part
    _finish_mixer(x, mix, 0, lng_ref, lnb_ref, rwt_ref, rb_ref, x1lin_ref, eidx_ref, wcol_ref)


def _mixer_out_shapes(t):
    return (jax.ShapeDtypeStruct((t * LANE_TILES, LANES), F32),
            jax.ShapeDtypeStruct((2, t), jnp.int32),
            jax.ShapeDtypeStruct((t, LANES), F32))


def _mixer_out_specs(tm):
    return (pl.BlockSpec((tm * LANE_TILES, LANES), lambda i: (i, 0)),
            pl.BlockSpec((2, tm), lambda i: (0, i)),
            pl.BlockSpec((tm, LANES), lambda i: (i, 0)))


def _conv_mixer(x, seq, slot, w_in, conv_w, w_out, ln_g, ln_b, rwt, rb):
    t = x.shape[0]
    tm = TM_CONV
    return pl.pallas_call(
        functools.partial(_conv_mixer_kernel, seq // tm),
        out_shape=_mixer_out_shapes(t),
        grid=(t // tm,),
        in_specs=[pl.BlockSpec((tm, D_MODEL), lambda i: (i, 0)),
                  _slot_spec(w_in.shape, slot), _const_spec(conv_w.shape),
                  _slot_spec(w_out.shape, slot),
                  _const_spec(ln_g.shape), _const_spec(ln_b.shape),
                  _const_spec(rwt.shape), _const_spec(rb.shape)],
        out_specs=_mixer_out_specs(tm),
        scratch_shapes=[pltpu.VMEM((SUBLANES, D_MODEL), F32)],
        compiler_params=pltpu.CompilerParams(dimension_semantics=("arbitrary",),
                                             vmem_limit_bytes=VMEM_LIMIT_BYTES),
        name="conv_mixer",
    )(x, w_in, conv_w, w_out, ln_g, ln_b, rwt, rb)


def _rotary(t, cos, sin):
    half = t.shape[-1] // 2
    t1, t2 = t[:, :half], t[:, half:]
    return jnp.concatenate([t1 * cos - t2 * sin, t2 * cos + t1 * sin], axis=-1)


def _retention_kernel(chunks_per_seq, x_ref, win_ref, wout_ref, cos_ref, sin_ref, dmask_ref,
                      qdec_ref, kdec_ref, cdec_ref, lng_ref, lnb_ref, rwt_ref, rb_ref,
                      x1lin_ref, eidx_ref, wcol_ref, state_ref):
    i = pl.program_id(0)

    @pl.when(i % chunks_per_seq == 0)
    def _():
        state_ref[...] = jnp.zeros_like(state_ref)

    tm = x_ref.shape[0]
    x = x_ref[...]
    xb = x.astype(BF16)
    cos, sin = cos_ref[...], sin_ref[...]

    def proj(lo, width):
        return jnp.dot(xb, win_ref[0, :, lo:lo + width], preferred_element_type=F32)

    mix = None
    for h in range(RET_HEADS):
        q = _rotary(proj(h * RET_DK, RET_DK), cos, sin).astype(BF16)
        k = _rotary(proj(D_MODEL + h * RET_DK, RET_DK) * (RET_DK ** -0.5), cos, sin)
        v = proj(2 * D_MODEL + h * RET_DV, RET_DV).astype(BF16)
        g = proj(2 * D_MODEL + RET_VDIM + h * RET_DV, RET_DV)
        outs = []
        for r0 in range(0, tm, RET_CHUNK):
            rows = slice(r0, r0 + RET_CHUNK)
            qb, kc, vb = q[rows], k[rows], v[rows]
            scores = lax.dot_general(qb, kc.astype(BF16), (((1,), (1,)), ((), ())),
                                     preferred_element_type=F32) * dmask_ref[h]
            inner = jnp.dot(scores.astype(BF16), vb, preferred_element_type=F32)
            state = state_ref[h]
            cross = jnp.dot(qb, state.astype(BF16), preferred_element_type=F32) * qdec_ref[h]
            kd = (kc * kdec_ref[h]).astype(BF16)
            state_ref[h] = state * cdec_ref[h] + lax.dot_general(
                kd, vb, (((0,), (0,)), ((), ())), preferred_element_type=F32)
            o = inner + cross
            mu = jnp.mean(o, axis=-1, keepdims=True)
            oc = o - mu
            outs.append(oc * lax.rsqrt(jnp.mean(oc * oc, axis=-1, keepdims=True) + LN_EPS))
        gated = (jax.nn.silu(g) * jnp.concatenate(outs, axis=0)).astype(BF16)
        part = jnp.dot(gated, wout_ref[0, h * RET_DV:(h + 1) * RET_DV, :],
                       preferred_element_type=F32)
        mix = part if mix is None else mix + part
    _finish_mixer(x, mix, 0, lng_ref, lnb_ref, rwt_ref, rb_ref, x1lin_ref, eidx_ref, wcol_ref)


def _retention_tables(seq):
    c = RET_CHUNK
    pos = jnp.arange(seq, dtype=F32)
    inv_freq = 1.0 / (ROPE_BASE ** jnp.linspace(0.0, 1.0, RET_DK // 2, dtype=F32))
    ang = pos[:, None] * inv_freq[None, :]
    log_gamma = jnp.log(1.0 - 2.0 ** (-5.0 - jnp.arange(RET_HEADS, dtype=F32)))
    idx = jnp.arange(c, dtype=F32)
    diff = idx[:, None] - idx[None, :]
    dmask = jnp.where(diff >= 0, jnp.exp(log_gamma[:, None, None] * jnp.maximum(diff, 0.0)), 0.0)
    qdec = jnp.exp(log_gamma[:, None] * (idx[None, :] + 1.0))[:, :, None]
    kdec = jnp.exp(log_gamma[:, None] * (c - 1.0 - idx[None, :]))[:, :, None]
    cdec = jnp.broadcast_to(jnp.exp(log_gamma * c)[:, None, None], (RET_HEADS, 1, RET_DV))
    return jnp.cos(ang), jnp.sin(ang), dmask, qdec, kdec, cdec


def _retention_mixer(x, seq, slot, w_in, w_out, ln_g, ln_b, rwt, rb):
    t = x.shape[0]
    c = TM_RET
    cps = seq // c
    cos, sin, dmask, qdec, kdec, cdec = _retention_tables(seq)
    return pl.pallas_call(
        functools.partial(_retention_kernel, cps),
        out_shape=_mixer_out_shapes(t),
        grid=(t // c,),
        in_specs=[pl.BlockSpec((c, D_MODEL), lambda i: (i, 0)),
                  _slot_spec(w_in.shape, slot), _slot_spec(w_out.shape, slot),
                  pl.BlockSpec((c, RET_DK // 2), lambda i: (i % cps, 0)),
                  pl.BlockSpec((c, RET_DK // 2), lambda i: (i % cps, 0)),
                  _const_spec(dmask.shape), _const_spec(qdec.shape), _const_spec(kdec.shape),
                  _const_spec(cdec.shape),
                  _const_spec(ln_g.shape), _const_spec(ln_b.shape),
                  _const_spec(rwt.shape), _const_spec(rb.shape)],
        out_specs=_mixer_out_specs(c),
        scratch_shapes=[pltpu.VMEM((RET_HEADS, RET_DK, RET_DV), F32)],
        compiler_params=pltpu.CompilerParams(dimension_semantics=("arbitrary",),
                                             vmem_limit_bytes=VMEM_LIMIT_BYTES),
        name="retention_mixer",
    )(x, w_in, w_out, cos, sin, dmask, qdec, kdec, cdec, ln_g, ln_b, rwt, rb)


def _sgu_kernel(x_ref, win_ref, sg_ref, sb_ref, ws_ref, bs_ref, wout_ref, lng_ref, lnb_ref,
                rwt_ref, rb_ref, x1lin_ref, eidx_ref, wcol_ref):
    c = SGU_CHUNK
    row = lax.broadcasted_iota(jnp.int32, (c, c), 0)
    col = lax.broadcasted_iota(jnp.int32, (c, c), 1)
    ws = [jnp.where(row >= col, ws_ref[g], 0.0).astype(BF16) for g in range(SGU_GROUPS)]
    def gelu_proj(xb, lo):
        zin = jnp.dot(xb, win_ref[0, :, lo:lo + SGU_GD], preferred_element_type=F32)
        return 0.5 * zin * (1.0 + lax.erf(zin * (2.0 ** -0.5)))

    tm = x_ref.shape[0]
    x = x_ref[...]
    xb = x.astype(BF16)
    v = jnp.concatenate([gelu_proj(xb, SGU_HALF + g * SGU_GD) for g in range(SGU_GROUPS)],
                        axis=-1)
    v = _layer_norm(v, sg_ref[...], sb_ref[...]).astype(BF16)
    mix = None
    for g in range(SGU_GROUPS):
        u = gelu_proj(xb, g * SGU_GD)
        vs = jnp.concatenate(
            [jnp.dot(ws[g], v[r0:r0 + c, g * SGU_GD:(g + 1) * SGU_GD],
                     preferred_element_type=F32) + bs_ref[g] for r0 in range(0, tm, c)], axis=0)
        part = jnp.dot((u * vs).astype(BF16), wout_ref[0, g * SGU_GD:(g + 1) * SGU_GD, :],
                       preferred_element_type=F32)
        mix = part if mix is None else mix + part
    _finish_mixer(x, mix, 0, lng_ref, lnb_ref, rwt_ref, rb_ref, x1lin_ref, eidx_ref, wcol_ref)


def _sgu_mixer(x, slot, w_in, sgu_g, sgu_b, w_s, b_s, w_out, ln_g, ln_b, rwt, rb):
    t = x.shape[0]
    c = TM_SGU
    return pl.pallas_call(
        _sgu_kernel,
        out_shape=_mixer_out_shapes(t),
        grid=(t // c,),
        in_specs=[pl.BlockSpec((c, D_MODEL), lambda i: (i, 0)),
                  _slot_spec(w_in.shape, slot), _const_spec(sgu_g.shape), _const_spec(sgu_b.shape),
                  _const_spec(w_s.shape), _const_spec(b_s.shape), _slot_spec(w_out.shape, slot),
                  _const_spec(ln_g.shape), _const_spec(ln_b.shape),
                  _const_spec(rwt.shape), _const_spec(rb.shape)],
        out_specs=_mixer_out_specs(c),
        compiler_params=pltpu.CompilerParams(dimension_semantics=("arbitrary",),
                                             vmem_limit_bytes=VMEM_LIMIT_BYTES),
        name="sgu_mixer",
    )(x, w_in, sgu_g, sgu_b, w_s, b_s, w_out, ln_g, ln_b, rwt, rb)


PLAN_BL = 256


def _plan_kernel(eidx_ref, dest_ref, offs_ref):
    tb = eidx_ref.shape[1]
    ns = 2 * tb
    e_all = jnp.concatenate([eidx_ref[0:1, :], eidx_ref[1:2, :]], axis=1)
    eid = lax.broadcasted_iota(jnp.int32, (N_EXPERTS, ns), 0)
    mask = eid == e_all
    off = jnp.sum(jnp.where(e_all < eid, 1.0, 0.0), axis=1, keepdims=True)
    r = lax.broadcasted_iota(jnp.int32, (PLAN_BL, PLAN_BL), 0)
    c = lax.broadcasted_iota(jnp.int32, (PLAN_BL, PLAN_BL), 1)
    upper = jnp.where(r < c, 1.0, 0.0).astype(BF16)
    carry = off
    pieces = []
    for b in range(ns // PLAN_BL):
        mb = mask[:, b * PLAN_BL:(b + 1) * PLAN_BL]
        mbf = jnp.where(mb, 1.0, 0.0)
        pos = jnp.dot(mbf.astype(BF16), upper, preferred_element_type=F32) + carry
        pieces.append(jnp.sum(jnp.where(mb, pos, 0.0), axis=0, keepdims=True))
        carry = carry + jnp.sum(mbf, axis=1, keepdims=True)
    dest_ref[0] = jnp.concatenate(pieces, axis=1).astype(jnp.int32)
    sub = lax.broadcasted_iota(jnp.int32, (N_EXPERTS, LANES), 0)
    lane = lax.broadcasted_iota(jnp.int32, (N_EXPERTS, LANES), 1)
    offs_row = jnp.sum(jnp.where(sub == lane, off, 0.0), axis=0, keepdims=True)
    lane1 = lax.broadcasted_iota(jnp.int32, (1, LANES), 1)
    offs_ref[0] = jnp.where(lane1 == N_EXPERTS, float(ns), offs_row).astype(jnp.int32)


def _plan(eidx, tb):
    t = eidx.shape[1]
    nb = t // tb
    return pl.pallas_call(
        _plan_kernel,
        out_shape=(jax.ShapeDtypeStruct((nb, 1, 2 * tb), jnp.int32),
                   jax.ShapeDtypeStruct((nb, 1, LANES), jnp.int32)),
        grid=(nb,),
        in_specs=[pl.BlockSpec((2, tb), lambda i: (0, i))],
        out_specs=(pl.BlockSpec((1, 1, 2 * tb), lambda i: (i, 0, 0)),
                   pl.BlockSpec((1, 1, LANES), lambda i: (i, 0, 0))),
        compiler_params=pltpu.CompilerParams(dimension_semantics=("arbitrary",)),
        name="moe_plan",
    )(eidx)


def _expert_ffn(g_ref, o_ref, wg_ref, wu_ref, wd_ref):
    ch = g_ref.shape[0] // LANE_TILES
    xg = jnp.concatenate([g_ref[pl.ds(cc, ch, stride=LANE_TILES), :]
                          for cc in range(LANE_TILES)], axis=1).astype(BF16)
    hg = jnp.dot(xg, wg_ref[0], preferred_element_type=F32)
    hu = jnp.dot(xg, wu_ref[0], preferred_element_type=F32)
    he = (jax.nn.silu(hg) * hu).astype(BF16)
    o = jnp.dot(he, wd_ref[0], preferred_element_type=F32)
    for cc in range(LANE_TILES):
        o_ref[pl.ds(cc, ch, stride=LANE_TILES), :] = o[:, cc * LANES:(cc + 1) * LANES]


def _moe_kernel(dest_ref, offs_ref, z_ref, wcol_ref, wg_ref, wu_ref, wd_ref, y_ref,
                inv_ref, g_ref, o_ref, gx_ref, ox_ref, y2_ref):
    i = pl.program_id(0)
    e = pl.program_id(1)
    tb = y_ref.shape[0]
    ns = 2 * tb
    ch = g_ref.shape[1] // LANE_TILES
    chx = gx_ref.shape[0] // LANE_TILES
    tok_mask = tb * LANE_TILES - 1
    rows_at = lambda off: pl.ds(pl.multiple_of(off, LANE_TILES), LANE_TILES)
    cur = e % 2
    nxt = 1 - cur

    @pl.when((i == 0) & (e == 0))
    def _():
        o_ref[...] = jnp.zeros_like(o_ref)

    @pl.when(e == 0)
    def _():
        def pad_body(k, carry):
            inv_ref[ns + k] = ns * LANE_TILES
            return carry
        lax.fori_loop(0, ch, pad_body, 0)

        def inv_body(k, carry):
            for u in range(ROW_UNROLL):
                s = k * ROW_UNROLL + u
                inv_ref[dest_ref[0, 0, s]] = s * LANE_TILES
            return carry
        lax.fori_loop(0, ns // ROW_UNROLL, inv_body, 0)

        def gather_body(k, carry):
            for u in range(ROW_UNROLL):
                j = k * ROW_UNROLL + u
                g_ref[0, rows_at(j * LANE_TILES), :] = z_ref[rows_at(inv_ref[j] & tok_mask), :]
            return carry
        lax.fori_loop(0, ch // ROW_UNROLL, gather_body, 0)

    start = offs_ref[0, 0, e]
    end = offs_ref[0, 0, e + 1]
    prev_start = jnp.where(e == 0, ns, offs_ref[0, 0, jnp.maximum(e - 1, 0)])

    for j in range(ch):
        y2_ref[rows_at(inv_ref[prev_start + j]), :] = o_ref[nxt, j * LANE_TILES:(j + 1) * LANE_TILES, :]
    _expert_ffn(g_ref.at[cur], o_ref.at[cur], wg_ref, wu_ref, wd_ref)
    for j in range(ch):
        g_ref[nxt, j * LANE_TILES:(j + 1) * LANE_TILES, :] = (
            z_ref[rows_at(inv_ref[end + j] & tok_mask), :])

    def extra_body(c, carry):
        a = start + ch + c * chx

        def gather_body(k, carry2):
            for u in range(SUBLANES):
                j = k * SUBLANES + u
                gx_ref[rows_at(j * LANE_TILES), :] = z_ref[rows_at(inv_ref[a + j] & tok_mask), :]
            return carry2
        lax.fori_loop(0, chx // SUBLANES, gather_body, 0)
        _expert_ffn(gx_ref, ox_ref, wg_ref, wu_ref, wd_ref)

        def scatter_body(k, carry2):
            for u in range(SUBLANES):
                j = k * SUBLANES + u
                y2_ref[rows_at(inv_ref[a + j]), :] = ox_ref[rows_at(j * LANE_TILES), :]
            return carry2
        lax.fori_loop(0, chx // SUBLANES, scatter_body, 0)
        return carry
    lax.fori_loop(0, (jnp.maximum(end - start - ch, 0) + (chx - 1)) // chx, extra_body, 0)

    @pl.when(e == N_EXPERTS - 1)
    def _():
        def scatter_body(k, carry):
            for u in range(ROW_UNROLL):
                j = k * ROW_UNROLL + u
                y2_ref[rows_at(inv_ref[start + j]), :] = o_ref[cur, rows_at(j * LANE_TILES), :]
            return carry
        lax.fori_loop(0, ch // ROW_UNROLL, scatter_body, 0)
        w = wcol_ref[...]
        w1 = jnp.broadcast_to(w[:, 0:1], (tb, LANES))
        w2 = jnp.broadcast_to(w[:, 1:2], (tb, LANES))
        for cc in range(LANE_TILES):
            y_first = y2_ref[pl.ds(cc, tb, stride=LANE_TILES), :]
            y_second = y2_ref[pl.ds(tb * LANE_TILES + cc, tb, stride=LANE_TILES), :]
            y_ref[:, cc * LANES:(cc + 1) * LANES] = w1 * y_first + w2 * y_second


def _moe(x1lin, eidx, wcol, layer, w_gate, w_up, w_down):
    t = wcol.shape[0]
    tb, ch = TB_MOE, CH_MOE
    dest, offs = _plan(eidx, tb)
    smem_spec = lambda n: pl.BlockSpec((1, 1, n), lambda i, e: (i, 0, 0), memory_space=pltpu.SMEM)
    expert_block = lambda i, e: (layer * N_EXPERTS + e, 0, 0)
    return pl.pallas_call(
        _moe_kernel,
        out_shape=jax.ShapeDtypeStruct((t, D_MODEL), F32),
        grid=(t // tb, N_EXPERTS),
        in_specs=[smem_spec(2 * tb), smem_spec(LANES),
                  pl.BlockSpec((tb * LANE_TILES, LANES), lambda i, e: (i, 0),
                               pipeline_mode=pl.Buffered(1)),
                  pl.BlockSpec((tb, LANES), lambda i, e: (i, 0)),
                  pl.BlockSpec((1, D_MODEL, D_EXPERT), expert_block),
                  pl.BlockSpec((1, D_MODEL, D_EXPERT), expert_block),
                  pl.BlockSpec((1, D_EXPERT, D_MODEL), expert_block)],
        out_specs=pl.BlockSpec((tb, D_MODEL), lambda i, e: (i, 0), pipeline_mode=pl.Buffered(1)),
        scratch_shapes=[pltpu.SMEM((2 * tb + ch,), jnp.int32),
                        pltpu.VMEM((2, ch * LANE_TILES, LANES), F32),
                        pltpu.VMEM((2, ch * LANE_TILES, LANES), F32),
                        pltpu.VMEM((CHX_MOE * LANE_TILES, LANES), F32),
                        pltpu.VMEM((CHX_MOE * LANE_TILES, LANES), F32),
                        pltpu.VMEM(((2 * tb + 1) * LANE_TILES, LANES), F32)],
        compiler_params=pltpu.CompilerParams(dimension_semantics=("arbitrary", "arbitrary"),
                                             vmem_limit_bytes=VMEM_LIMIT_BYTES),
        name="moe",
    )(dest, offs, x1lin, wcol, w_gate, w_up, w_down)


def _post_kernel(x1lin_ref, y_ref, p_ref, lng_ref, lnb_ref, wp_ref, wg_ref, bg_ref, o_ref):
    tm = y_ref.shape[0]
    x1 = jnp.concatenate([x1lin_ref[pl.ds(c, tm, stride=LANE_TILES), :]
                          for c in range(LANE_TILES)], axis=1)
    x2 = _layer_norm(ALPHA * x1 + y_ref[...], lng_ref[...], lnb_ref[...])
    emb = jnp.dot(p_ref[0].astype(BF16), wp_ref[0], preferred_element_type=F32)
    gate = jax.nn.sigmoid(
        jnp.dot(x2.astype(BF16), wg_ref[0], preferred_element_type=F32) + bg_ref[...])
    o_ref[...] = x2 + gate * emb


def _post(x1lin, y, layer, p, ln_g, ln_b, w_proj, w_gate, b_gate):
    t = y.shape[0]
    tm = TM_POST
    return pl.pallas_call(
        _post_kernel,
        out_shape=jax.ShapeDtypeStruct((t, D_MODEL), F32),
        grid=(t // tm,),
        in_specs=[pl.BlockSpec((tm * LANE_TILES, LANES), lambda i: (i, 0)),
                  pl.BlockSpec((tm, D_MODEL), lambda i: (i, 0)),
                  pl.BlockSpec((1, tm, PLE_DIM), lambda i: (layer, i, 0)),
                  _const_spec(ln_g.shape), _const_spec(ln_b.shape),
                  _slot_spec(w_proj.shape, layer), _slot_spec(w_gate.shape, layer),
                  _const_spec(b_gate.shape)],
        out_specs=pl.BlockSpec((tm, D_MODEL), lambda i: (i, 0)),
        compiler_params=pltpu.CompilerParams(dimension_semantics=("arbitrary",),
                                             vmem_limit_bytes=VMEM_LIMIT_BYTES),
        name="post_moe",
    )(x1lin, y, p, ln_g, ln_b, w_proj, w_gate, b_gate)


def kernel(x, p, a_w_in, a_conv_w, a_w_out, b_w_in, b_w_out, c_w_in, c_ln_g, c_ln_b, c_w_s, c_b_s,
           c_w_out, router_w, router_b, moe_w_gate, moe_w_up, moe_w_down, ln_g, ln_b, ple_w_proj,
           ple_w_gate, ple_b_gate):
    bsz, seq, d = x.shape
    t = bsz * seq
    xt = x.reshape(t, d)
    pt = p.reshape(DEPTH, t, PLE_DIM)
    rwt = router_w.T.astype(BF16)
    rb = router_b.astype(F32).reshape(N_EXPERTS, 1)
    row = lambda a: a.reshape(1, -1)
    a_w_in_b, a_w_out_b = a_w_in.astype(BF16), a_w_out.astype(BF16)
    moe_shape = lambda w: w.astype(BF16).reshape((DEPTH * N_EXPERTS,) + w.shape[2:])
    moe_wg, moe_wu, moe_wd = moe_shape(moe_w_gate), moe_shape(moe_w_up), moe_shape(moe_w_down)
    ple_wp, ple_wg = ple_w_proj.astype(BF16), ple_w_gate.astype(BF16)
    for i in range(DEPTH):
        mixer, slot = i % N_MIXERS, i // N_MIXERS
        g1, b1 = row(ln_g[i, 0]), row(ln_b[i, 0])
        if mixer == 0:
            outs = _conv_mixer(xt, seq, slot, a_w_in_b, a_conv_w[slot], a_w_out_b, g1, b1, rwt, rb)
        elif mixer == 1:
            outs = _retention_mixer(xt, seq, slot, b_w_in.astype(BF16), b_w_out.astype(BF16), g1,
                                    b1, rwt, rb)
        else:
            outs = _sgu_mixer(xt, slot, c_w_in.astype(BF16), row(c_ln_g[slot]), row(c_ln_b[slot]),
                              c_w_s[slot], c_b_s[slot][:, :, None], c_w_out.astype(BF16), g1, b1,
                              rwt, rb)
        x1lin, eidx, wcol = outs
        y = _moe(x1lin, eidx, wcol, i, moe_wg, moe_wu, moe_wd)
        xt = _post(x1lin, y, i, pt, row(ln_g[i, 1]), row(ln_b[i, 1]), ple_wp, ple_wg,
                   row(ple_b_gate[i]))
    return xt.reshape(bsz, seq, d)
```

```python
import functools
import math

import numpy as np
import jax
import jax.numpy as jnp
from jax import lax
from jax.experimental import pallas as pl
from jax.experimental.pallas import tpu as pltpu

F32 = jnp.float32
BF16 = jnp.bfloat16

D_MODEL = 1024
DEPTH = 4
N_MIXERS = 3
CONV_WIDTH = 3
RET_HEADS = 4
RET_DK = D_MODEL // RET_HEADS
RET_DV = 2 * RET_DK
RET_VDIM = RET_HEADS * RET_DV
RET_CHUNK = 128
ROPE_BASE = 10000.0
SGU_HALF = 3 * D_MODEL
SGU_GROUPS = 4
SGU_GD = SGU_HALF // SGU_GROUPS
SGU_CHUNK = 128
N_EXPERTS = 16
N_GROUPS = 4
EXPERTS_PER_GROUP = N_EXPERTS // N_GROUPS
D_EXPERT = D_MODEL // 2
PLE_DIM = 256
ALPHA = (2 * DEPTH) ** 0.25
LN_EPS = 1e-5

VMEM_LIMIT_BYTES = 56 * 1024 * 1024
SUBLANES = 8
LANES = 128
LANE_TILES = D_MODEL // LANES

TM_CONV = 1024
TM_CONV_FUSED = 1024
CONV_COLS = 1024
TM_RET = 512
TM_SGU = 512
TM_POST = 1024
TB_MOE = 2048
CH_MOE = 320
CHX_MOE = 64
EXPERTS_PER_STEP = 2
ROW_UNROLL = 16


def _const_spec(shape):
    nd = len(shape)
    return pl.BlockSpec(shape, lambda *_: (0,) * nd, pipeline_mode=pl.Buffered(1))


def _slot_spec(shape, slot):
    nd = len(shape)
    return pl.BlockSpec((1,) + tuple(shape[1:]), lambda *_: (slot,) + (0,) * (nd - 1),
                        pipeline_mode=pl.Buffered(1))


def _layer_norm(h, g, b):
    mu = jnp.mean(h, axis=-1, keepdims=True)
    hc = h - mu
    var = jnp.mean(hc * hc, axis=-1, keepdims=True)
    return hc * lax.rsqrt(var + LN_EPS) * g + b


def _top2_sum(a, b, c, d):
    m1, n1 = jnp.maximum(a, b), jnp.minimum(a, b)
    m2, n2 = jnp.maximum(c, d), jnp.minimum(c, d)
    return jnp.maximum(m1, m2) + jnp.maximum(jnp.minimum(m1, m2), jnp.maximum(n1, n2))


def _route(x1b, rwt_ref, rb_ref):
    tm = x1b.shape[0]
    logits_t = lax.dot_general(rwt_ref[...], x1b, (((1,), (1,)), ((), ())),
                               preferred_element_type=F32)
    scores = jax.nn.sigmoid(logits_t)
    sel = scores + rb_ref[...]
    s = [sel[e:e + 1, :] for e in range(N_EXPERTS)]
    sc = [scores[e:e + 1, :] for e in range(N_EXPERTS)]
    grp = [_top2_sum(*s[4 * g:4 * g + 4]) for g in range(N_GROUPS)]
    best, g_idx = grp[0], jnp.zeros((1, tm), jnp.int32)
    for g in range(1, N_GROUPS):
        better = grp[g] > best
        best = jnp.where(better, grp[g], best)
        g_idx = jnp.where(better, g, g_idx)

    def pick(rows, j):
        out = rows[j]
        for g in range(1, N_GROUPS):
            out = jnp.where(g_idx == g, rows[4 * g + j], out)
        return out
    v = [pick(s, j) for j in range(EXPERTS_PER_GROUP)]
    w = [pick(sc, j) for j in range(EXPERTS_PER_GROUP)]

    def argmax4(vals):
        bv, bi = vals[0], jnp.zeros((1, tm), jnp.int32)
        for j in range(1, EXPERTS_PER_GROUP):
            better = vals[j] > bv
            bv = jnp.where(better, vals[j], bv)
            bi = jnp.where(better, j, bi)
        return bi
    i1 = argmax4(v)
    i2 = argmax4([jnp.where(i1 == j, -jnp.inf, v[j]) for j in range(EXPERTS_PER_GROUP)])

    def take(vals, idx):
        out = vals[0]
        for j in range(1, EXPERTS_PER_GROUP):
            out = jnp.where(idx == j, vals[j], out)
        return out
    w1, w2 = take(w, i1), take(w, i2)
    den = w1 + w2
    w1, w2 = w1 / den, w2 / den
    e1, e2 = g_idx * EXPERTS_PER_GROUP + i1, g_idx * EXPERTS_PER_GROUP + i2
    return e1, e2, w1, w2


def _finish_mixer(x, mix, r0, lng_ref, lnb_ref, rwt_ref, rb_ref, x1lin_ref, eidx_ref,
                  wcol_ref):
    tm = x.shape[0]
    x1 = _layer_norm(ALPHA * x + mix, lng_ref[...], lnb_ref[...])
    for c in range(LANE_TILES):
        x1lin_ref[pl.ds(r0 * LANE_TILES + c, tm, stride=LANE_TILES), :] = (
            x1[:, c * LANES:(c + 1) * LANES])
    e1, e2, w1, w2 = _route(x1.astype(BF16), rwt_ref, rb_ref)
    eidx_ref[:, r0:r0 + tm] = jnp.concatenate([e1, e2], axis=0)
    pad = jnp.zeros((LANES - 2, tm), F32)
    wcol_ref[r0:r0 + tm, :] = jnp.concatenate([w1, w2, pad], axis=0).T


def _conv_mixer_kernel(n_src, tiles_per_seq, *refs):
    src = refs[:n_src]
    (win_ref, cw_ref, wout_ref, lng_ref, lnb_ref, rwt_ref, rb_ref, x1lin_ref, eidx_ref, wcol_ref,
     carry_ref) = refs[n_src:]
    i = pl.program_id(0)
    tm = wcol_ref.shape[0]
    cg = CONV_COLS

    @pl.when(i % tiles_per_seq == 0)
    def _():
        carry_ref[...] = jnp.zeros_like(carry_ref)
    row = lax.broadcasted_iota(jnp.int32, (tm, 1), 0)
    x = _load_tile(src)
    xb = x.astype(BF16)

    def proj(lo):
        return jnp.dot(xb, win_ref[0, :, lo:lo + cg], preferred_element_type=F32)

    mix = None
    for lo in range(0, D_MODEL, cg):
        b_gate = proj(lo)
        z = proj(D_MODEL + lo) * proj(2 * D_MODEL + lo)
        prev = carry_ref[:, lo:lo + cg]
        p1, p2 = prev[SUBLANES - 1:SUBLANES], prev[SUBLANES - 2:SUBLANES - 1]
        z1 = jnp.where(row == 0, p1, pltpu.roll(z, 1, axis=0))
        z2 = jnp.where(row == 0, p2, jnp.where(row == 1, p1, pltpu.roll(z, 2, axis=0)))
        carry_ref[:, lo:lo + cg] = z[tm - SUBLANES:, :]
        cw = cw_ref[:, lo:lo + cg]
        zc = cw[0:1] * z2 + cw[1:2] * z1 + cw[2:3] * z
        part = jnp.dot((b_gate * zc).astype(BF16), wout_ref[0, lo:lo + cg, :],
                       preferred_element_type=F32)
        mix = part if mix is None else mix + part
    _finish_mixer(x, mix, 0, lng_ref, lnb_ref, rwt_ref, rb_ref, x1lin_ref, eidx_ref, wcol_ref)


def _mixer_out_shapes(t):
    return (jax.ShapeDtypeStruct((t * LANE_TILES, LANES), F32),
            jax.ShapeDtypeStruct((2, t), jnp.int32),
            jax.ShapeDtypeStruct((t, LANES), F32))


def _mixer_out_specs(tm):
    return (pl.BlockSpec((tm * LANE_TILES, LANES), lambda i: (i, 0)),
            pl.BlockSpec((2, tm), lambda i: (0, i)),
            pl.BlockSpec((tm, LANES), lambda i: (i, 0)))


def _conv_mixer(src, t, seq, slot, w_in, conv_w, w_out, ln_g, ln_b, rwt, rb):
    arrays, specs = src
    tm = TM_CONV if len(arrays) == 1 else TM_CONV_FUSED
    return pl.pallas_call(
        functools.partial(_conv_mixer_kernel, len(arrays), seq // tm),
        out_shape=_mixer_out_shapes(t),
        grid=(t // tm,),
        in_specs=[*specs(tm),
                  _slot_spec(w_in.shape, slot), _const_spec(conv_w.shape),
                  _slot_spec(w_out.shape, slot),
                  _const_spec(ln_g.shape), _const_spec(ln_b.shape),
                  _const_spec(rwt.shape), _const_spec(rb.shape)],
        out_specs=_mixer_out_specs(tm),
        scratch_shapes=[pltpu.VMEM((SUBLANES, D_MODEL), F32)],
        compiler_params=pltpu.CompilerParams(dimension_semantics=("arbitrary",),
                                             vmem_limit_bytes=VMEM_LIMIT_BYTES),
        name="conv_mixer",
    )(*arrays, w_in, conv_w, w_out, ln_g, ln_b, rwt, rb)


def _rotary(t, cos, sin):
    half = t.shape[-1] // 2
    t1, t2 = t[:, :half], t[:, half:]
    return jnp.concatenate([t1 * cos - t2 * sin, t2 * cos + t1 * sin], axis=-1)


def _retention_kernel(n_src, chunks_per_seq, *refs):
    src = refs[:n_src]
    (win_ref, wout_ref, cos_ref, sin_ref, dmask_ref, qdec_ref, kdec_ref, cdec_ref, lng_ref,
     lnb_ref, rwt_ref, rb_ref, x1lin_ref, eidx_ref, wcol_ref, state_ref) = refs[n_src:]
    i = pl.program_id(0)

    @pl.when(i % chunks_per_seq == 0)
    def _():
        state_ref[...] = jnp.zeros_like(state_ref)

    tm = wcol_ref.shape[0]
    x = _load_tile(src)
    xb = x.astype(BF16)
    cos, sin = cos_ref[...], sin_ref[...]

    def proj(lo, width):
        return jnp.dot(xb, win_ref[0, :, lo:lo + width], preferred_element_type=F32)

    mix = None
    for h in range(RET_HEADS):
        q = _rotary(proj(h * RET_DK, RET_DK), cos, sin).astype(BF16)
        k = _rotary(proj(D_MODEL + h * RET_DK, RET_DK) * (RET_DK ** -0.5), cos, sin)
        v = proj(2 * D_MODEL + h * RET_DV, RET_DV).astype(BF16)
        g = proj(2 * D_MODEL + RET_VDIM + h * RET_DV, RET_DV)
        outs = []
        for r0 in range(0, tm, RET_CHUNK):
            rows = slice(r0, r0 + RET_CHUNK)
            qb, kc, vb = q[rows], k[rows], v[rows]
            scores = lax.dot_general(qb, kc.astype(BF16), (((1,), (1,)), ((), ())),
                                     preferred_element_type=F32) * dmask_ref[h]
            inner = jnp.dot(scores.astype(BF16), vb, preferred_element_type=F32)
            state = state_ref[h]
            cross = jnp.dot(qb, state.astype(BF16), preferred_element_type=F32) * qdec_ref[h]
            kd = (kc * kdec_ref[h]).astype(BF16)
            state_ref[h] = state * cdec_ref[h] + lax.dot_general(
                kd, vb, (((0,), (0,)), ((), ())), preferred_element_type=F32)
            o = inner + cross
            mu = jnp.mean(o, axis=-1, keepdims=True)
            oc = o - mu
            outs.append(oc * lax.rsqrt(jnp.mean(oc * oc, axis=-1, keepdims=True) + LN_EPS))
        gated = (jax.nn.silu(g) * jnp.concatenate(outs, axis=0)).astype(BF16)
        part = jnp.dot(gated, wout_ref[0, h * RET_DV:(h + 1) * RET_DV, :],
                       preferred_element_type=F32)
        mix = part if mix is None else mix + part
    _finish_mixer(x, mix, 0, lng_ref, lnb_ref, rwt_ref, rb_ref, x1lin_ref, eidx_ref, wcol_ref)


def _retention_tables(seq):
    c = RET_CHUNK
    pos = jnp.arange(seq, dtype=F32)
    inv_freq = 1.0 / (ROPE_BASE ** jnp.linspace(0.0, 1.0, RET_DK // 2, dtype=F32))
    ang = pos[:, None] * inv_freq[None, :]
    log_gamma = jnp.log(1.0 - 2.0 ** (-5.0 - jnp.arange(RET_HEADS, dtype=F32)))
    idx = jnp.arange(c, dtype=F32)
    diff = idx[:, None] - idx[None, :]
    dmask = jnp.where(diff >= 0, jnp.exp(log_gamma[:, None, None] * jnp.maximum(diff, 0.0)), 0.0)
    qdec = jnp.exp(log_gamma[:, None] * (idx[None, :] + 1.0))[:, :, None]
    kdec = jnp.exp(log_gamma[:, None] * (c - 1.0 - idx[None, :]))[:, :, None]
    cdec = jnp.broadcast_to(jnp.exp(log_gamma * c)[:, None, None], (RET_HEADS, 1, RET_DV))
    return jnp.cos(ang), jnp.sin(ang), dmask, qdec, kdec, cdec


def _retention_mixer(src, t, seq, slot, w_in, w_out, ln_g, ln_b, rwt, rb):
    arrays, specs = src
    c = TM_RET
    cps = seq // c
    cos, sin, dmask, qdec, kdec, cdec = _retention_tables(seq)
    return pl.pallas_call(
        functools.partial(_retention_kernel, len(arrays), cps),
        out_shape=_mixer_out_shapes(t),
        grid=(t // c,),
        in_specs=[*specs(c),
                  _slot_spec(w_in.shape, slot), _slot_spec(w_out.shape, slot),
                  pl.BlockSpec((c, RET_DK // 2), lambda i: (i % cps, 0)),
                  pl.BlockSpec((c, RET_DK // 2), lambda i: (i % cps, 0)),
                  _const_spec(dmask.shape), _const_spec(qdec.shape), _const_spec(kdec.shape),
                  _const_spec(cdec.shape),
                  _const_spec(ln_g.shape), _const_spec(ln_b.shape),
                  _const_spec(rwt.shape), _const_spec(rb.shape)],
        out_specs=_mixer_out_specs(c),
        scratch_shapes=[pltpu.VMEM((RET_HEADS, RET_DK, RET_DV), F32)],
        compiler_params=pltpu.CompilerParams(dimension_semantics=("arbitrary",),
                                             vmem_limit_bytes=VMEM_LIMIT_BYTES),
        name="retention_mixer",
    )(*arrays, w_in, w_out, cos, sin, dmask, qdec, kdec, cdec, ln_g, ln_b, rwt, rb)


def _sgu_kernel(n_src, *refs):
    src = refs[:n_src]
    (win_ref, sg_ref, sb_ref, ws_ref, bs_ref, wout_ref, lng_ref, lnb_ref, rwt_ref, rb_ref,
     x1lin_ref, eidx_ref, wcol_ref) = refs[n_src:]
    c = SGU_CHUNK
    row = lax.broadcasted_iota(jnp.int32, (c, c), 0)
    col = lax.broadcasted_iota(jnp.int32, (c, c), 1)
    ws = [jnp.where(row >= col, ws_ref[g], 0.0).astype(BF16) for g in range(SGU_GROUPS)]
    def gelu_proj(xb, lo):
        zin = jnp.dot(xb, win_ref[0, :, lo:lo + SGU_GD], preferred_element_type=F32)
        return 0.5 * zin * (1.0 + lax.erf(zin * (2.0 ** -0.5)))

    tm = wcol_ref.shape[0]
    x = _load_tile(src)
    xb = x.astype(BF16)
    v = jnp.concatenate([gelu_proj(xb, SGU_HALF + g * SGU_GD) for g in range(SGU_GROUPS)],
                        axis=-1)
    v = _layer_norm(v, sg_ref[...], sb_ref[...]).astype(BF16)
    mix = None
    for g in range(SGU_GROUPS):
        u = gelu_proj(xb, g * SGU_GD)
        vs = jnp.concatenate(
            [jnp.dot(ws[g], v[r0:r0 + c, g * SGU_GD:(g + 1) * SGU_GD],
                     preferred_element_type=F32) + bs_ref[g] for r0 in range(0, tm, c)], axis=0)
        part = jnp.dot((u * vs).astype(BF16), wout_ref[0, g * SGU_GD:(g + 1) * SGU_GD, :],
                       preferred_element_type=F32)
        mix = part if mix is None else mix + part
    _finish_mixer(x, mix, 0, lng_ref, lnb_ref, rwt_ref, rb_ref, x1lin_ref, eidx_ref, wcol_ref)


def _sgu_mixer(src, t, slot, w_in, sgu_g, sgu_b, w_s, b_s, w_out, ln_g, ln_b, rwt, rb):
    arrays, specs = src
    c = TM_SGU
    return pl.pallas_call(
        functools.partial(_sgu_kernel, len(arrays)),
        out_shape=_mixer_out_shapes(t),
        grid=(t // c,),
        in_specs=[*specs(c),
                  _slot_spec(w_in.shape, slot), _const_spec(sgu_g.shape), _const_spec(sgu_b.shape),
                  _const_spec(w_s.shape), _const_spec(b_s.shape), _slot_spec(w_out.shape, slot),
                  _const_spec(ln_g.shape), _const_spec(ln_b.shape),
                  _const_spec(rwt.shape), _const_spec(rb.shape)],
        out_specs=_mixer_out_specs(c),
        compiler_params=pltpu.CompilerParams(dimension_semantics=("arbitrary",),
                                             vmem_limit_bytes=VMEM_LIMIT_BYTES),
        name="sgu_mixer",
    )(*arrays, w_in, sgu_g, sgu_b, w_s, b_s, w_out, ln_g, ln_b, rwt, rb)


PLAN_BL = 256


def _plan_kernel(eidx_ref, dest_ref, offs_ref):
    tb = eidx_ref.shape[1]
    ns = 2 * tb
    e_all = jnp.concatenate([eidx_ref[0:1, :], eidx_ref[1:2, :]], axis=1)
    eid = lax.broadcasted_iota(jnp.int32, (N_EXPERTS, ns), 0)
    mask = eid == e_all
    off = jnp.sum(jnp.where(e_all < eid, 1.0, 0.0), axis=1, keepdims=True)
    r = lax.broadcasted_iota(jnp.int32, (PLAN_BL, PLAN_BL), 0)
    c = lax.broadcasted_iota(jnp.int32, (PLAN_BL, PLAN_BL), 1)
    upper = jnp.where(r < c, 1.0, 0.0).astype(BF16)
    carry = off
    pieces = []
    for b in range(ns // PLAN_BL):
        mb = mask[:, b * PLAN_BL:(b + 1) * PLAN_BL]
        mbf = jnp.where(mb, 1.0, 0.0)
        pos = jnp.dot(mbf.astype(BF16), upper, preferred_element_type=F32) + carry
        pieces.append(jnp.sum(jnp.where(mb, pos, 0.0), axis=0, keepdims=True))
        carry = carry + jnp.sum(mbf, axis=1, keepdims=True)
    dest_ref[0] = jnp.concatenate(pieces, axis=1).astype(jnp.int32)
    sub = lax.broadcasted_iota(jnp.int32, (N_EXPERTS, LANES), 0)
    lane = lax.broadcasted_iota(jnp.int32, (N_EXPERTS, LANES), 1)
    offs_row = jnp.sum(jnp.where(sub == lane, off, 0.0), axis=0, keepdims=True)
    lane1 = lax.broadcasted_iota(jnp.int32, (1, LANES), 1)
    offs_ref[0] = jnp.where(lane1 == N_EXPERTS, float(ns), offs_row).astype(jnp.int32)


def _plan(eidx, tb):
    t = eidx.shape[1]
    nb = t // tb
    return pl.pallas_call(
        _plan_kernel,
        out_shape=(jax.ShapeDtypeStruct((nb, 1, 2 * tb), jnp.int32),
                   jax.ShapeDtypeStruct((nb, 1, LANES), jnp.int32)),
        grid=(nb,),
        in_specs=[pl.BlockSpec((2, tb), lambda i: (0, i))],
        out_specs=(pl.BlockSpec((1, 1, 2 * tb), lambda i: (i, 0, 0)),
                   pl.BlockSpec((1, 1, LANES), lambda i: (i, 0, 0))),
        compiler_params=pltpu.CompilerParams(dimension_semantics=("arbitrary",)),
        name="moe_plan",
    )(eidx)


def _expert_ffn(g_ref, o_ref, q, wg_ref, wu_ref, wd_ref):
    ch = g_ref.shape[0] // LANE_TILES
    xg = jnp.concatenate([g_ref[pl.ds(cc, ch, stride=LANE_TILES), :]
                          for cc in range(LANE_TILES)], axis=1).astype(BF16)
    hg = jnp.dot(xg, wg_ref[q], preferred_element_type=F32)
    hu = jnp.dot(xg, wu_ref[q], preferred_element_type=F32)
    he = (jax.nn.silu(hg) * hu).astype(BF16)
    o = jnp.dot(he, wd_ref[q], preferred_element_type=F32)
    for cc in range(LANE_TILES):
        o_ref[pl.ds(cc, ch, stride=LANE_TILES), :] = o[:, cc * LANES:(cc + 1) * LANES]


def _moe_kernel(dest_ref, offs_ref, z_ref, wcol_ref, wg_ref, wu_ref, wd_ref, y_ref,
                inv_ref, g_ref, o_ref, gx_ref, ox_ref, y2_ref):
    i = pl.program_id(0)
    k = pl.program_id(1)
    eps = wg_ref.shape[0]
    e0 = k * eps
    tb = y_ref.shape[0]
    ns = 2 * tb
    ch = g_ref.shape[1] // LANE_TILES
    chx = gx_ref.shape[0] // LANE_TILES
    tok_mask = tb * LANE_TILES - 1
    rows_at = lambda off: pl.ds(pl.multiple_of(off, LANE_TILES), LANE_TILES)

    @pl.when((i == 0) & (k == 0))
    def _():
        o_ref[...] = jnp.zeros_like(o_ref)

    @pl.when(k == 0)
    def _():
        def pad_body(k, carry):
            inv_ref[ns + k] = ns * LANE_TILES
            return carry
        lax.fori_loop(0, ch, pad_body, 0)

        def inv_body(k, carry):
            for u in range(ROW_UNROLL):
                s = k * ROW_UNROLL + u
                inv_ref[dest_ref[0, 0, s]] = s * LANE_TILES
            return carry
        lax.fori_loop(0, ns // ROW_UNROLL, inv_body, 0)

        def gather_body(k, carry):
            for u in range(ROW_UNROLL):
                j = k * ROW_UNROLL + u
                g_ref[0, rows_at(j * LANE_TILES), :] = z_ref[rows_at(inv_ref[j] & tok_mask), :]
            return carry
        lax.fori_loop(0, ch // ROW_UNROLL, gather_body, 0)

    bounds = [offs_ref[0, 0, e0 + q] for q in range(eps + 1)]
    prev_start = jnp.where(k == 0, ns, offs_ref[0, 0, jnp.maximum(e0 - 1, 0)])

    for q in range(eps):
        cur, nxt = q % 2, 1 - q % 2
        scatter_base = prev_start if q == 0 else bounds[q - 1]
        for j in range(ch):
            y2_ref[rows_at(inv_ref[scatter_base + j]), :] = (
                o_ref[nxt, j * LANE_TILES:(j + 1) * LANE_TILES, :])
        _expert_ffn(g_ref.at[cur], o_ref.at[cur], q, wg_ref, wu_ref, wd_ref)
        for j in range(ch):
            g_ref[nxt, j * LANE_TILES:(j + 1) * LANE_TILES, :] = (
                z_ref[rows_at(inv_ref[bounds[q + 1] + j] & tok_mask), :])

    for q in range(eps):
        start, end = bounds[q], bounds[q + 1]

        def extra_body(c, carry, start=start, q=q):
            a = start + ch + c * chx

            def gather_body(r, carry2):
                for u in range(SUBLANES):
                    j = r * SUBLANES + u
                    gx_ref[rows_at(j * LANE_TILES), :] = (
                        z_ref[rows_at(inv_ref[a + j] & tok_mask), :])
                return carry2
            lax.fori_loop(0, chx // SUBLANES, gather_body, 0)
            _expert_ffn(gx_ref, ox_ref, q, wg_ref, wu_ref, wd_ref)

            def scatter_body(r, carry2):
                for u in range(SUBLANES):
                    j = r * SUBLANES + u
                    y2_ref[rows_at(inv_ref[a + j]), :] = ox_ref[rows_at(j * LANE_TILES), :]
                return carry2
            lax.fori_loop(0, chx // SUBLANES, scatter_body, 0)
            return carry
        lax.fori_loop(0, (jnp.maximum(end - start - ch, 0) + (chx - 1)) // chx, extra_body, 0)

    @pl.when(k == pl.num_programs(1) - 1)
    def _():
        last = (eps - 1) % 2

        def scatter_body(r, carry):
            for u in range(ROW_UNROLL):
                j = r * ROW_UNROLL + u
                y2_ref[rows_at(inv_ref[bounds[eps - 1] + j]), :] = (
                    o_ref[last, rows_at(j * LANE_TILES), :])
            return carry
        lax.fori_loop(0, ch // ROW_UNROLL, scatter_body, 0)
        w = wcol_ref[...]
        w1 = jnp.broadcast_to(w[:, 0:1], (tb, LANES))
        w2 = jnp.broadcast_to(w[:, 1:2], (tb, LANES))
        for cc in range(LANE_TILES):
            y_first = y2_ref[pl.ds(cc, tb, stride=LANE_TILES), :]
            y_second = y2_ref[pl.ds(tb * LANE_TILES + cc, tb, stride=LANE_TILES), :]
            y_ref[:, cc * LANES:(cc + 1) * LANES] = (w1 * y_first + w2 * y_second).astype(y_ref.dtype)


def _moe(x1lin, eidx, wcol, layer, w_gate, w_up, w_down):
    t = wcol.shape[0]
    tb, ch = TB_MOE, CH_MOE
    dest, offs = _plan(eidx, tb)
    eps = EXPERTS_PER_STEP
    assert eps % 2 == 0 and N_EXPERTS % eps == 0
    steps = N_EXPERTS // eps
    smem_spec = lambda n: pl.BlockSpec((1, 1, n), lambda i, e: (i, 0, 0), memory_space=pltpu.SMEM)
    expert_block = lambda i, k: (layer * steps + k, 0, 0)
    return pl.pallas_call(
        _moe_kernel,
        out_shape=jax.ShapeDtypeStruct((t, D_MODEL), BF16),
        grid=(t // tb, steps),
        in_specs=[smem_spec(2 * tb), smem_spec(LANES),
                  pl.BlockSpec((tb * LANE_TILES, LANES), lambda i, e: (i, 0),
                               pipeline_mode=pl.Buffered(1)),
                  pl.BlockSpec((tb, LANES), lambda i, e: (i, 0)),
                  pl.BlockSpec((eps, D_MODEL, D_EXPERT), expert_block),
                  pl.BlockSpec((eps, D_MODEL, D_EXPERT), expert_block),
                  pl.BlockSpec((eps, D_EXPERT, D_MODEL), expert_block)],
        out_specs=pl.BlockSpec((tb, D_MODEL), lambda i, e: (i, 0), pipeline_mode=pl.Buffered(1)),
        scratch_shapes=[pltpu.SMEM((2 * tb + ch,), jnp.int32),
                        pltpu.VMEM((2, ch * LANE_TILES, LANES), F32),
                        pltpu.VMEM((2, ch * LANE_TILES, LANES), F32),
                        pltpu.VMEM((CHX_MOE * LANE_TILES, LANES), F32),
                        pltpu.VMEM((CHX_MOE * LANE_TILES, LANES), F32),
                        pltpu.VMEM(((2 * tb + 1) * LANE_TILES, LANES), F32)],
        compiler_params=pltpu.CompilerParams(dimension_semantics=("arbitrary", "arbitrary"),
                                             vmem_limit_bytes=VMEM_LIMIT_BYTES),
        name="moe",
    )(dest, offs, x1lin, wcol, w_gate, w_up, w_down)


def _post_tile(x1lin_ref, y_ref, p_ref, lng_ref, lnb_ref, wp_ref, wg_ref, bg_ref):
    tm = y_ref.shape[0]
    x1 = jnp.concatenate([x1lin_ref[pl.ds(c, tm, stride=LANE_TILES), :]
                          for c in range(LANE_TILES)], axis=1)
    x2 = _layer_norm(ALPHA * x1 + y_ref[...].astype(F32), lng_ref[...], lnb_ref[...])
    emb = jnp.dot(p_ref[0].astype(BF16), wp_ref[0], preferred_element_type=F32)
    gate = jax.nn.sigmoid(
        jnp.dot(x2.astype(BF16), wg_ref[0], preferred_element_type=F32) + bg_ref[...])
    return x2 + gate * emb


def _load_tile(src):
    return src[0][...] if len(src) == 1 else _post_tile(*src)


def _plain_source(x):
    return (x,), lambda tm: [pl.BlockSpec((tm, D_MODEL), lambda i: (i, 0))]


def _post_source(x1lin, y, layer, p, ln_g, ln_b, w_proj, w_gate, b_gate):
    specs = lambda tm: [pl.BlockSpec((tm * LANE_TILES, LANES), lambda i: (i, 0)),
                        pl.BlockSpec((tm, D_MODEL), lambda i: (i, 0)),
                        pl.BlockSpec((1, tm, PLE_DIM), lambda i: (layer, i, 0)),
                        _const_spec(ln_g.shape), _const_spec(ln_b.shape),
                        _slot_spec(w_proj.shape, layer), _slot_spec(w_gate.shape, layer),
                        _const_spec(b_gate.shape)]
    return (x1lin, y, p, ln_g, ln_b, w_proj, w_gate, b_gate), specs


def _post_kernel(*refs):
    refs[-1][...] = _post_tile(*refs[:-1])


def _post(src, t):
    arrays, specs = src
    tm = TM_POST
    return pl.pallas_call(
        _post_kernel,
        out_shape=jax.ShapeDtypeStruct((t, D_MODEL), F32),
        grid=(t // tm,),
        in_specs=specs(tm),
        out_specs=pl.BlockSpec((tm, D_MODEL), lambda i: (i, 0)),
        compiler_params=pltpu.CompilerParams(dimension_semantics=("arbitrary",),
                                             vmem_limit_bytes=VMEM_LIMIT_BYTES),
        name="post_moe",
    )(*arrays)


def kernel(x, p, a_w_in, a_conv_w, a_w_out, b_w_in, b_w_out, c_w_in, c_ln_g, c_ln_b, c_w_s, c_b_s,
           c_w_out, router_w, router_b, moe_w_gate, moe_w_up, moe_w_down, ln_g, ln_b, ple_w_proj,
           ple_w_gate, ple_b_gate):
    bsz, seq, d = x.shape
    t = bsz * seq
    xt = x.reshape(t, d)
    pt = p.reshape(DEPTH, t, PLE_DIM)
    rwt = router_w.T.astype(BF16)
    rb = router_b.astype(F32).reshape(N_EXPERTS, 1)
    row = lambda a: a.reshape(1, -1)
    a_w_in_b, a_w_out_b = a_w_in.astype(BF16), a_w_out.astype(BF16)
    moe_shape = lambda w: w.astype(BF16).reshape((DEPTH * N_EXPERTS,) + w.shape[2:])
    moe_wg, moe_wu, moe_wd = moe_shape(moe_w_gate), moe_shape(moe_w_up), moe_shape(moe_w_down)
    ple_wp, ple_wg = ple_w_proj.astype(BF16), ple_w_gate.astype(BF16)
    src = _plain_source(xt)
    for i in range(DEPTH):
        mixer, slot = i % N_MIXERS, i // N_MIXERS
        g1, b1 = row(ln_g[i, 0]), row(ln_b[i, 0])
        if mixer == 0:
            outs = _conv_mixer(src, t, seq, slot, a_w_in_b, a_conv_w[slot], a_w_out_b, g1, b1, rwt,
                               rb)
        elif mixer == 1:
            outs = _retention_mixer(src, t, seq, slot, b_w_in.astype(BF16), b_w_out.astype(BF16),
                                    g1, b1, rwt, rb)
        else:
            outs = _sgu_mixer(src, t, slot, c_w_in.astype(BF16), row(c_ln_g[slot]),
                              row(c_ln_b[slot]), c_w_s[slot], c_b_s[slot][:, :, None],
                              c_w_out.astype(BF16), g1, b1, rwt, rb)
        x1lin, eidx, wcol = outs
        y = _moe(x1lin, eidx, wcol, i, moe_wg, moe_wu, moe_wd)
        src = _post_source(x1lin, y, i, pt, row(ln_g[i, 1]), row(ln_b[i, 1]), ple_wp, ple_wg,
                           row(ple_b_gate[i]))
    return _post(src, t).reshape(bsz, seq, d)
```

```python
import functools

import numpy as np
import jax
import jax.numpy as jnp
from jax import lax
from jax.experimental import pallas as pl
from jax.experimental.pallas import tpu as pltpu

F32 = jnp.float32
BF16 = jnp.bfloat16

D_MODEL = 1024
DEPTH = 4
N_MIXERS = 3
CONV_WIDTH = 3
RET_HEADS = 4
RET_DK = D_MODEL // RET_HEADS
RET_DV = 2 * RET_DK
RET_VDIM = RET_HEADS * RET_DV
RET_CHUNK = 256
ROPE_BASE = 10000.0
SGU_HALF = 3 * D_MODEL
SGU_GROUPS = 4
SGU_GD = SGU_HALF // SGU_GROUPS
SGU_CHUNK = 128
N_EXPERTS = 16
N_GROUPS = 4
EXPERTS_PER_GROUP = N_EXPERTS // N_GROUPS
D_EXPERT = D_MODEL // 2
PLE_DIM = 256
ALPHA = (2 * DEPTH) ** 0.25
LN_EPS = 1e-5

VMEM_LIMIT_BYTES = 56 * 1024 * 1024
SUBLANES = 8
LANES = 128
LANE_TILES = D_MODEL // LANES

TM_CONV = 1024
TM_CONV_FUSED = 1024
CONV_COLS = 1024
TM_RET = 512
TM_SGU = 512
TM_POST = 1024
TB_MOE = 2048
CH_MOE = 320
CHX_MOE = 64
EXPERTS_PER_STEP = 2
ROW_UNROLL = 16


def _const_spec(shape):
    nd = len(shape)
    return pl.BlockSpec(shape, lambda *_: (0,) * nd, pipeline_mode=pl.Buffered(1))


def _slot_spec(shape, slot):
    nd = len(shape)
    return pl.BlockSpec((1,) + tuple(shape[1:]), lambda *_: (slot,) + (0,) * (nd - 1),
                        pipeline_mode=pl.Buffered(1))


def _layer_norm(h, g, b):
    mu = jnp.mean(h, axis=-1, keepdims=True)
    hc = h - mu
    var = jnp.mean(hc * hc, axis=-1, keepdims=True)
    return hc * lax.rsqrt(var + LN_EPS) * g + b


def _top2_sum(a, b, c, d):
    m1, n1 = jnp.maximum(a, b), jnp.minimum(a, b)
    m2, n2 = jnp.maximum(c, d), jnp.minimum(c, d)
    return jnp.maximum(m1, m2) + jnp.maximum(jnp.minimum(m1, m2), jnp.maximum(n1, n2))


def _route(x1b, rwt_ref, rb_ref):
    tm = x1b.shape[0]
    logits_t = lax.dot_general(rwt_ref[...], x1b, (((1,), (1,)), ((), ())),
                               preferred_element_type=F32)
    scores = jax.nn.sigmoid(logits_t)
    sel = scores + rb_ref[...]
    s = [sel[e:e + 1, :] for e in range(N_EXPERTS)]
    sc = [scores[e:e + 1, :] for e in range(N_EXPERTS)]
    grp = [_top2_sum(*s[EXPERTS_PER_GROUP * g:EXPERTS_PER_GROUP * (g + 1)])
           for g in range(N_GROUPS)]
    best, g_idx = grp[0], jnp.zeros((1, tm), jnp.int32)
    for g in range(1, N_GROUPS):
        better = grp[g] > best
        best = jnp.where(better, grp[g], best)
        g_idx = jnp.where(better, g, g_idx)

    def pick(rows, j):
        out = rows[j]
        for g in range(1, N_GROUPS):
            out = jnp.where(g_idx == g, rows[EXPERTS_PER_GROUP * g + j], out)
        return out
    v = [pick(s, j) for j in range(EXPERTS_PER_GROUP)]
    w = [pick(sc, j) for j in range(EXPERTS_PER_GROUP)]

    def argmax4(vals):
        bv, bi = vals[0], jnp.zeros((1, tm), jnp.int32)
        for j in range(1, EXPERTS_PER_GROUP):
            better = vals[j] > bv
            bv = jnp.where(better, vals[j], bv)
            bi = jnp.where(better, j, bi)
        return bi
    i1 = argmax4(v)
    i2 = argmax4([jnp.where(i1 == j, -jnp.inf, v[j]) for j in range(EXPERTS_PER_GROUP)])

    def take(vals, idx):
        out = vals[0]
        for j in range(1, EXPERTS_PER_GROUP):
            out = jnp.where(idx == j, vals[j], out)
        return out
    w1, w2 = take(w, i1), take(w, i2)
    den = w1 + w2
    w1, w2 = w1 / den, w2 / den
    e1, e2 = g_idx * EXPERTS_PER_GROUP + i1, g_idx * EXPERTS_PER_GROUP + i2
    return e1, e2, w1, w2


def _finish_mixer(x, mix, r0, lng_ref, lnb_ref, rwt_ref, rb_ref, x1lin_ref, eidx_ref,
                  wcol_ref):
    tm = x.shape[0]
    x1 = _layer_norm(ALPHA * x + mix, lng_ref[...], lnb_ref[...])
    for c in range(LANE_TILES):
        x1lin_ref[pl.ds(r0 * LANE_TILES + c, tm, stride=LANE_TILES), :] = (
            x1[:, c * LANES:(c + 1) * LANES])
    e1, e2, w1, w2 = _route(x1.astype(BF16), rwt_ref, rb_ref)
    eidx_ref[:, r0:r0 + tm] = jnp.concatenate([e1, e2], axis=0)
    pad = jnp.zeros((LANES - 2, tm), F32)
    wcol_ref[r0:r0 + tm, :] = jnp.concatenate([w1, w2, pad], axis=0).T


def _conv_mixer_kernel(n_src, tiles_per_seq, *refs):
    src = refs[:n_src]
    (win_ref, cw_ref, wout_ref, lng_ref, lnb_ref, rwt_ref, rb_ref, x1lin_ref, eidx_ref, wcol_ref,
     carry_ref) = refs[n_src:]
    i = pl.program_id(0)
    tm = wcol_ref.shape[0]
    cg = CONV_COLS

    @pl.when(i % tiles_per_seq == 0)
    def _():
        carry_ref[...] = jnp.zeros_like(carry_ref)
    row = lax.broadcasted_iota(jnp.int32, (tm, 1), 0)
    x = _load_tile(src)
    xb = x.astype(BF16)

    def proj(lo):
        return jnp.dot(xb, win_ref[0, :, lo:lo + cg], preferred_element_type=F32)

    mix = None
    for lo in range(0, D_MODEL, cg):
        b_gate = proj(lo)
        z = proj(D_MODEL + lo) * proj(2 * D_MODEL + lo)
        prev = carry_ref[:, lo:lo + cg]
        p1, p2 = prev[SUBLANES - 1:SUBLANES], prev[SUBLANES - 2:SUBLANES - 1]
        z1 = jnp.where(row == 0, p1, pltpu.roll(z, 1, axis=0))
        z2 = jnp.where(row == 0, p2, jnp.where(row == 1, p1, pltpu.roll(z, 2, axis=0)))
        carry_ref[:, lo:lo + cg] = z[tm - SUBLANES:, :]
        cw = cw_ref[:, lo:lo + cg]
        zc = cw[0:1] * z2 + cw[1:2] * z1 + cw[2:3] * z
        part = jnp.dot((b_gate * zc).astype(BF16), wout_ref[0, lo:lo + cg, :],
                       preferred_element_type=F32)
        mix = part if mix is None else mix + part
    _finish_mixer(x, mix, 0, lng_ref, lnb_ref, rwt_ref, rb_ref, x1lin_ref, eidx_ref, wcol_ref)


def _mixer_out_shapes(t):
    return (jax.ShapeDtypeStruct((t * LANE_TILES, LANES), F32),
            jax.ShapeDtypeStruct((2, t), jnp.int32),
            jax.ShapeDtypeStruct((t, LANES), F32))


def _mixer_out_specs(tm):
    return (pl.BlockSpec((tm * LANE_TILES, LANES), lambda i: (i, 0)),
            pl.BlockSpec((2, tm), lambda i: (0, i)),
            pl.BlockSpec((tm, LANES), lambda i: (i, 0)))


def _conv_mixer(src, t, seq, slot, w_in, conv_w, w_out, ln_g, ln_b, rwt, rb):
    arrays, specs = src
    tm = TM_CONV if len(arrays) == 1 else TM_CONV_FUSED
    return pl.pallas_call(
        functools.partial(_conv_mixer_kernel, len(arrays), seq // tm),
        out_shape=_mixer_out_shapes(t),
        grid=(t // tm,),
        in_specs=[*specs(tm),
                  _slot_spec(w_in.shape, slot), _const_spec(conv_w.shape),
                  _slot_spec(w_out.shape, slot),
                  _const_spec(ln_g.shape), _const_spec(ln_b.shape),
                  _const_spec(rwt.shape), _const_spec(rb.shape)],
        out_specs=_mixer_out_specs(tm),
        scratch_shapes=[pltpu.VMEM((SUBLANES, D_MODEL), F32)],
        compiler_params=pltpu.CompilerParams(dimension_semantics=("arbitrary",),
                                             vmem_limit_bytes=VMEM_LIMIT_BYTES),
        name="conv_mixer",
    )(*arrays, w_in, conv_w, w_out, ln_g, ln_b, rwt, rb)


def _rotary(t, cos, sin):
    half = t.shape[-1] // 2
    t1, t2 = t[:, :half], t[:, half:]
    return jnp.concatenate([t1 * cos - t2 * sin, t2 * cos + t1 * sin], axis=-1)


def _retention_kernel(n_src, chunks_per_seq, *refs):
    src = refs[:n_src]
    (win_ref, wout_ref, cos_ref, sin_ref, dmask_ref, qdec_ref, kdec_ref, cdec_ref, lng_ref,
     lnb_ref, rwt_ref, rb_ref, x1lin_ref, eidx_ref, wcol_ref, state_ref) = refs[n_src:]
    i = pl.program_id(0)

    @pl.when(i % chunks_per_seq == 0)
    def _():
        state_ref[...] = jnp.zeros_like(state_ref)

    tm = wcol_ref.shape[0]
    x = _load_tile(src)
    xb = x.astype(BF16)
    cos, sin = cos_ref[...], sin_ref[...]

    def proj(lo, width):
        return jnp.dot(xb, win_ref[0, :, lo:lo + width], preferred_element_type=F32)

    mix = None
    for h in range(RET_HEADS):
        q = _rotary(proj(h * RET_DK, RET_DK), cos, sin).astype(BF16)
        k = _rotary(proj(D_MODEL + h * RET_DK, RET_DK) * (RET_DK ** -0.5), cos, sin)
        v = proj(2 * D_MODEL + h * RET_DV, RET_DV).astype(BF16)
        g = proj(2 * D_MODEL + RET_VDIM + h * RET_DV, RET_DV)
        outs = []
        for r0 in range(0, tm, RET_CHUNK):
            rows = slice(r0, r0 + RET_CHUNK)
            qb, kc, vb = q[rows], k[rows], v[rows]
            scores = lax.dot_general(qb, kc.astype(BF16), (((1,), (1,)), ((), ())),
                                     preferred_element_type=F32) * dmask_ref[h]
            inner = jnp.dot(scores.astype(BF16), vb, preferred_element_type=F32)
            state = state_ref[h]
            cross = jnp.dot(qb, state.astype(BF16), preferred_element_type=F32) * qdec_ref[h]
            kd = (kc * kdec_ref[h]).astype(BF16)
            state_ref[h] = state * cdec_ref[h] + lax.dot_general(
                kd, vb, (((0,), (0,)), ((), ())), preferred_element_type=F32)
            o = inner + cross
            mu = jnp.mean(o, axis=-1, keepdims=True)
            oc = o - mu
            outs.append(oc * lax.rsqrt(jnp.mean(oc * oc, axis=-1, keepdims=True) + LN_EPS))
        gated = (jax.nn.silu(g) * jnp.concatenate(outs, axis=0)).astype(BF16)
        part = jnp.dot(gated, wout_ref[0, h * RET_DV:(h + 1) * RET_DV, :],
                       preferred_element_type=F32)
        mix = part if mix is None else mix + part
    _finish_mixer(x, mix, 0, lng_ref, lnb_ref, rwt_ref, rb_ref, x1lin_ref, eidx_ref, wcol_ref)


def _retention_tables(seq):
    c = RET_CHUNK
    pos = np.arange(seq, dtype=np.float64)
    inv_freq = 1.0 / (ROPE_BASE ** np.linspace(0.0, 1.0, RET_DK // 2))
    ang = pos[:, None] * inv_freq[None, :]
    log_gamma = np.log(1.0 - 2.0 ** (-5.0 - np.arange(RET_HEADS, dtype=np.float64)))
    idx = np.arange(c, dtype=np.float64)
    diff = idx[:, None] - idx[None, :]
    dmask = np.where(diff >= 0, np.exp(log_gamma[:, None, None] * np.maximum(diff, 0.0)), 0.0)
    qdec = np.exp(log_gamma[:, None] * (idx[None, :] + 1.0))[:, :, None]
    kdec = np.exp(log_gamma[:, None] * (c - 1.0 - idx[None, :]))[:, :, None]
    cdec = np.broadcast_to(np.exp(log_gamma * c)[:, None, None], (RET_HEADS, 1, RET_DV))
    return tuple(jnp.asarray(a, dtype=F32)
                 for a in (np.cos(ang), np.sin(ang), dmask, qdec, kdec, cdec))


def _retention_mixer(src, t, seq, slot, w_in, w_out, ln_g, ln_b, rwt, rb):
    arrays, specs = src
    c = TM_RET
    cps = seq // c
    cos, sin, dmask, qdec, kdec, cdec = _retention_tables(seq)
    return pl.pallas_call(
        functools.partial(_retention_kernel, len(arrays), cps),
        out_shape=_mixer_out_shapes(t),
        grid=(t // c,),
        in_specs=[*specs(c),
                  _slot_spec(w_in.shape, slot), _slot_spec(w_out.shape, slot),
                  pl.BlockSpec((c, RET_DK // 2), lambda i: (i % cps, 0)),
                  pl.BlockSpec((c, RET_DK // 2), lambda i: (i % cps, 0)),
                  _const_spec(dmask.shape), _const_spec(qdec.shape), _const_spec(kdec.shape),
                  _const_spec(cdec.shape),
                  _const_spec(ln_g.shape), _const_spec(ln_b.shape),
                  _const_spec(rwt.shape), _const_spec(rb.shape)],
        out_specs=_mixer_out_specs(c),
        scratch_shapes=[pltpu.VMEM((RET_HEADS, RET_DK, RET_DV), F32)],
        compiler_params=pltpu.CompilerParams(dimension_semantics=("arbitrary",),
                                             vmem_limit_bytes=VMEM_LIMIT_BYTES),
        name="retention_mixer",
    )(*arrays, w_in, w_out, cos, sin, dmask, qdec, kdec, cdec, ln_g, ln_b, rwt, rb)


def _sgu_kernel(n_src, *refs):
    src = refs[:n_src]
    (win_ref, sg_ref, sb_ref, ws_ref, bs_ref, wout_ref, lng_ref, lnb_ref, rwt_ref, rb_ref,
     x1lin_ref, eidx_ref, wcol_ref) = refs[n_src:]
    c = SGU_CHUNK
    row = lax.broadcasted_iota(jnp.int32, (c, c), 0)
    col = lax.broadcasted_iota(jnp.int32, (c, c), 1)
    ws = [jnp.where(row >= col, ws_ref[g], 0.0).astype(BF16) for g in range(SGU_GROUPS)]

    def gelu_proj(xb, lo):
        zin = jnp.dot(xb, win_ref[0, :, lo:lo + SGU_GD], preferred_element_type=F32)
        return 0.5 * zin * (1.0 + lax.erf(zin * (2.0 ** -0.5)))

    tm = wcol_ref.shape[0]
    x = _load_tile(src)
    xb = x.astype(BF16)
    v = jnp.concatenate([gelu_proj(xb, SGU_HALF + g * SGU_GD) for g in range(SGU_GROUPS)],
                        axis=-1)
    v = _layer_norm(v, sg_ref[...], sb_ref[...]).astype(BF16)
    mix = None
    for g in range(SGU_GROUPS):
        u = gelu_proj(xb, g * SGU_GD)
        vs = jnp.concatenate(
            [jnp.dot(ws[g], v[r0:r0 + c, g * SGU_GD:(g + 1) * SGU_GD],
                     preferred_element_type=F32) + bs_ref[g] for r0 in range(0, tm, c)], axis=0)
        part = jnp.dot((u * vs).astype(BF16), wout_ref[0, g * SGU_GD:(g + 1) * SGU_GD, :],
                       preferred_element_type=F32)
        mix = part if mix is None else mix + part
    _finish_mixer(x, mix, 0, lng_ref, lnb_ref, rwt_ref, rb_ref, x1lin_ref, eidx_ref, wcol_ref)


def _sgu_mixer(src, t, slot, w_in, sgu_g, sgu_b, w_s, b_s, w_out, ln_g, ln_b, rwt, rb):
    arrays, specs = src
    c = TM_SGU
    return pl.pallas_call(
        functools.partial(_sgu_kernel, len(arrays)),
        out_shape=_mixer_out_shapes(t),
        grid=(t // c,),
        in_specs=[*specs(c),
                  _slot_spec(w_in.shape, slot), _const_spec(sgu_g.shape), _const_spec(sgu_b.shape),
                  _const_spec(w_s.shape), _const_spec(b_s.shape), _slot_spec(w_out.shape, slot),
                  _const_spec(ln_g.shape), _const_spec(ln_b.shape),
                  _const_spec(rwt.shape), _const_spec(rb.shape)],
        out_specs=_mixer_out_specs(c),
        compiler_params=pltpu.CompilerParams(dimension_semantics=("arbitrary",),
                                             vmem_limit_bytes=VMEM_LIMIT_BYTES),
        name="sgu_mixer",
    )(*arrays, w_in, sgu_g, sgu_b, w_s, b_s, w_out, ln_g, ln_b, rwt, rb)


PLAN_BL = 256


def _plan_kernel(eidx_ref, dest_ref, offs_ref):
    tb = eidx_ref.shape[1]
    ns = 2 * tb
    e_all = jnp.concatenate([eidx_ref[0:1, :], eidx_ref[1:2, :]], axis=1)
    eid = lax.broadcasted_iota(jnp.int32, (N_EXPERTS, ns), 0)
    mask = eid == e_all
    off = jnp.sum(jnp.where(e_all < eid, 1.0, 0.0), axis=1, keepdims=True)
    r = lax.broadcasted_iota(jnp.int32, (PLAN_BL, PLAN_BL), 0)
    c = lax.broadcasted_iota(jnp.int32, (PLAN_BL, PLAN_BL), 1)
    upper = jnp.where(r < c, 1.0, 0.0).astype(BF16)
    carry = off
    pieces = []
    for b in range(ns // PLAN_BL):
        mb = mask[:, b * PLAN_BL:(b + 1) * PLAN_BL]
        mbf = jnp.where(mb, 1.0, 0.0)
        pos = jnp.dot(mbf.astype(BF16), upper, preferred_element_type=F32) + carry
        pieces.append(jnp.sum(jnp.where(mb, pos, 0.0), axis=0, keepdims=True))
        carry = carry + jnp.sum(mbf, axis=1, keepdims=True)
    dest_ref[0] = jnp.concatenate(pieces, axis=1).astype(jnp.int32)
    sub = lax.broadcasted_iota(jnp.int32, (N_EXPERTS, LANES), 0)
    lane = lax.broadcasted_iota(jnp.int32, (N_EXPERTS, LANES), 1)
    offs_row = jnp.sum(jnp.where(sub == lane, off, 0.0), axis=0, keepdims=True)
    lane1 = lax.broadcasted_iota(jnp.int32, (1, LANES), 1)
    offs_ref[0] = jnp.where(lane1 == N_EXPERTS, float(ns), offs_row).astype(jnp.int32)


def _plan(eidx, tb):
    t = eidx.shape[1]
    nb = t // tb
    return pl.pallas_call(
        _plan_kernel,
        out_shape=(jax.ShapeDtypeStruct((nb, 1, 2 * tb), jnp.int32),
                   jax.ShapeDtypeStruct((nb, 1, LANES), jnp.int32)),
        grid=(nb,),
        in_specs=[pl.BlockSpec((2, tb), lambda i: (0, i))],
        out_specs=(pl.BlockSpec((1, 1, 2 * tb), lambda i: (i, 0, 0)),
                   pl.BlockSpec((1, 1, LANES), lambda i: (i, 0, 0))),
        compiler_params=pltpu.CompilerParams(dimension_semantics=("arbitrary",)),
        name="moe_plan",
    )(eidx)


def _expert_ffn(g_ref, o_ref, q, wg_ref, wu_ref, wd_ref):
    ch = g_ref.shape[0] // LANE_TILES
    xg = jnp.concatenate([g_ref[pl.ds(cc, ch, stride=LANE_TILES), :]
                          for cc in range(LANE_TILES)], axis=1).astype(BF16)
    hg = jnp.dot(xg, wg_ref[q], preferred_element_type=F32)
    hu = jnp.dot(xg, wu_ref[q], preferred_element_type=F32)
    he = (jax.nn.silu(hg) * hu).astype(BF16)
    o = jnp.dot(he, wd_ref[q], preferred_element_type=F32)
    for cc in range(LANE_TILES):
        o_ref[pl.ds(cc, ch, stride=LANE_TILES), :] = o[:, cc * LANES:(cc + 1) * LANES]


def _moe_kernel(dest_ref, offs_ref, z_ref, wcol_ref, wg_ref, wu_ref, wd_ref, y_ref,
                inv_ref, g_ref, o_ref, gx_ref, ox_ref, y2_ref):
    i = pl.program_id(0)
    k = pl.program_id(1)
    eps = wg_ref.shape[0]
    e0 = k * eps
    tb = y_ref.shape[0]
    ns = 2 * tb
    ch = g_ref.shape[1] // LANE_TILES
    chx = gx_ref.shape[0] // LANE_TILES
    tok_mask = tb * LANE_TILES - 1
    rows_at = lambda off: pl.ds(pl.multiple_of(off, LANE_TILES), LANE_TILES)

    @pl.when((i == 0) & (k == 0))
    def _():
        o_ref[...] = jnp.zeros_like(o_ref)

    @pl.when(k == 0)
    def _():
        def pad_body(r, carry):
            inv_ref[ns + r] = ns * LANE_TILES
            return carry
        lax.fori_loop(0, ch, pad_body, 0)

        def inv_body(r, carry):
            for u in range(ROW_UNROLL):
                s = r * ROW_UNROLL + u
                inv_ref[dest_ref[0, 0, s]] = s * LANE_TILES
            return carry
        lax.fori_loop(0, ns // ROW_UNROLL, inv_body, 0)

        def gather_body(r, carry):
            for u in range(ROW_UNROLL):
                j = r * ROW_UNROLL + u
                g_ref[0, rows_at(j * LANE_TILES), :] = z_ref[rows_at(inv_ref[j] & tok_mask), :]
            return carry
        lax.fori_loop(0, ch // ROW_UNROLL, gather_body, 0)

    bounds = [offs_ref[0, 0, e0 + q] for q in range(eps + 1)]
    prev_start = jnp.where(k == 0, ns, offs_ref[0, 0, jnp.maximum(e0 - 1, 0)])

    for q in range(eps):
        cur, nxt = q % 2, 1 - q % 2
        scatter_base = prev_start if q == 0 else bounds[q - 1]
        for j in range(ch):
            y2_ref[rows_at(inv_ref[scatter_base + j]), :] = (
                o_ref[nxt, j * LANE_TILES:(j + 1) * LANE_TILES, :])
        _expert_ffn(g_ref.at[cur], o_ref.at[cur], q, wg_ref, wu_ref, wd_ref)
        for j in range(ch):
            g_ref[nxt, j * LANE_TILES:(j + 1) * LANE_TILES, :] = (
                z_ref[rows_at(inv_ref[bounds[q + 1] + j] & tok_mask), :])

    for q in range(eps):
        start, end = bounds[q], bounds[q + 1]

        def extra_body(c, carry, start=start, q=q):
            a = start + ch + c * chx

            def gather_body(r, carry2):
                for u in range(SUBLANES):
                    j = r * SUBLANES + u
                    gx_ref[rows_at(j * LANE_TILES), :] = (
                        z_ref[rows_at(inv_ref[a + j] & tok_mask), :])
                return carry2
            lax.fori_loop(0, chx // SUBLANES, gather_body, 0)
            _expert_ffn(gx_ref, ox_ref, q, wg_ref, wu_ref, wd_ref)

            def scatter_body(r, carry2):
                for u in range(SUBLANES):
                    j = r * SUBLANES + u
                    y2_ref[rows_at(inv_ref[a + j]), :] = ox_ref[rows_at(j * LANE_TILES), :]
                return carry2
            lax.fori_loop(0, chx // SUBLANES, scatter_body, 0)
            return carry
        lax.fori_loop(0, (jnp.maximum(end - start - ch, 0) + (chx - 1)) // chx, extra_body, 0)

    @pl.when(k == pl.num_programs(1) - 1)
    def _():
        last = (eps - 1) % 2

        def scatter_body(r, carry):
            for u in range(ROW_UNROLL):
                j = r * ROW_UNROLL + u
                y2_ref[rows_at(inv_ref[bounds[eps - 1] + j]), :] = (
                    o_ref[last, rows_at(j * LANE_TILES), :])
            return carry
        lax.fori_loop(0, ch // ROW_UNROLL, scatter_body, 0)
        w = wcol_ref[...]
        w1 = jnp.broadcast_to(w[:, 0:1], (tb, LANES))
        w2 = jnp.broadcast_to(w[:, 1:2], (tb, LANES))
        for cc in range(LANE_TILES):
            y_first = y2_ref[pl.ds(cc, tb, stride=LANE_TILES), :]
            y_second = y2_ref[pl.ds(tb * LANE_TILES + cc, tb, stride=LANE_TILES), :]
            y_ref[:, cc * LANES:(cc + 1) * LANES] = (
                w1 * y_first + w2 * y_second).astype(y_ref.dtype)


def _moe(x1lin, eidx, wcol, layer, w_gate, w_up, w_down):
    t = wcol.shape[0]
    tb, ch = TB_MOE, CH_MOE
    dest, offs = _plan(eidx, tb)
    eps = EXPERTS_PER_STEP
    assert eps % 2 == 0 and N_EXPERTS % eps == 0
    steps = N_EXPERTS // eps
    smem_spec = lambda n: pl.BlockSpec((1, 1, n), lambda i, e: (i, 0, 0), memory_space=pltpu.SMEM)
    expert_block = lambda i, k: (layer * steps + k, 0, 0)
    return pl.pallas_call(
        _moe_kernel,
        out_shape=jax.ShapeDtypeStruct((t, D_MODEL), BF16),
        grid=(t // tb, steps),
        in_specs=[smem_spec(2 * tb), smem_spec(LANES),
                  pl.BlockSpec((tb * LANE_TILES, LANES), lambda i, e: (i, 0),
                               pipeline_mode=pl.Buffered(1)),
                  pl.BlockSpec((tb, LANES), lambda i, e: (i, 0)),
                  pl.BlockSpec((eps, D_MODEL, D_EXPERT), expert_block),
                  pl.BlockSpec((eps, D_MODEL, D_EXPERT), expert_block),
                  pl.BlockSpec((eps, D_EXPERT, D_MODEL), expert_block)],
        out_specs=pl.BlockSpec((tb, D_MODEL), lambda i, e: (i, 0), pipeline_mode=pl.Buffered(1)),
        scratch_shapes=[pltpu.SMEM((2 * tb + ch,), jnp.int32),
                        pltpu.VMEM((2, ch * LANE_TILES, LANES), F32),
                        pltpu.VMEM((2, ch * LANE_TILES, LANES), F32),
                        pltpu.VMEM((CHX_MOE * LANE_TILES, LANES), F32),
                        pltpu.VMEM((CHX_MOE * LANE_TILES, LANES), F32),
                        pltpu.VMEM(((2 * tb + 1) * LANE_TILES, LANES), F32)],
        compiler_params=pltpu.CompilerParams(dimension_semantics=("arbitrary", "arbitrary"),
                                             vmem_limit_bytes=VMEM_LIMIT_BYTES),
        name="moe",
    )(dest, offs, x1lin, wcol, w_gate, w_up, w_down)


def _post_tile(x1lin_ref, y_ref, p_ref, lng_ref, lnb_ref, wp_ref, wg_ref, bg_ref):
    tm = y_ref.shape[0]
    x1 = jnp.concatenate([x1lin_ref[pl.ds(c, tm, stride=LANE_TILES), :]
                          for c in range(LANE_TILES)], axis=1)
    x2 = _layer_norm(ALPHA * x1 + y_ref[...].astype(F32), lng_ref[...], lnb_ref[...])
    emb = jnp.dot(p_ref[0].astype(BF16), wp_ref[0], preferred_element_type=F32)
    gate = jax.nn.sigmoid(
        jnp.dot(x2.astype(BF16), wg_ref[0], preferred_element_type=F32) + bg_ref[...])
    return x2 + gate * emb


def _load_tile(src):
    return src[0][...] if len(src) == 1 else _post_tile(*src)


def _plain_source(x):
    return (x,), lambda tm: [pl.BlockSpec((tm, D_MODEL), lambda i: (i, 0))]


def _post_source(x1lin, y, layer, p, ln_g, ln_b, w_proj, w_gate, b_gate):
    specs = lambda tm: [pl.BlockSpec((tm * LANE_TILES, LANES), lambda i: (i, 0)),
                        pl.BlockSpec((tm, D_MODEL), lambda i: (i, 0)),
                        pl.BlockSpec((1, tm, PLE_DIM), lambda i: (layer, i, 0)),
                        _const_spec(ln_g.shape), _const_spec(ln_b.shape),
                        _slot_spec(w_proj.shape, layer), _slot_spec(w_gate.shape, layer),
                        _const_spec(b_gate.shape)]
    return (x1lin, y, p, ln_g, ln_b, w_proj, w_gate, b_gate), specs


def _post_kernel(*refs):
    refs[-1][...] = _post_tile(*refs[:-1])


def _post(src, t):
    arrays, specs = src
    tm = TM_POST
    return pl.pallas_call(
        _post_kernel,
        out_shape=jax.ShapeDtypeStruct((t, D_MODEL), F32),
        grid=(t // tm,),
        in_specs=specs(tm),
        out_specs=pl.BlockSpec((tm, D_MODEL), lambda i: (i, 0)),
        compiler_params=pltpu.CompilerParams(dimension_semantics=("arbitrary",),
                                             vmem_limit_bytes=VMEM_LIMIT_BYTES),
        name="post_moe",
    )(*arrays)


def kernel(x, p, a_w_in, a_conv_w, a_w_out, b_w_in, b_w_out, c_w_in, c_ln_g, c_ln_b, c_w_s, c_b_s,
           c_w_out, router_w, router_b, moe_w_gate, moe_w_up, moe_w_down, ln_g, ln_b, ple_w_proj,
           ple_w_gate, ple_b_gate):
    bsz, seq, d = x.shape
    t = bsz * seq
    xt = x.reshape(t, d)
    pt = p.reshape(DEPTH, t, PLE_DIM)
    rwt = router_w.T.astype(BF16)
    rb = router_b.astype(F32).reshape(N_EXPERTS, 1)
    row = lambda a: a.reshape(1, -1)
    a_w_in_b, a_w_out_b = a_w_in.astype(BF16), a_w_out.astype(BF16)
    moe_shape = lambda w: w.astype(BF16).reshape((DEPTH * N_EXPERTS,) + w.shape[2:])
    moe_wg, moe_wu, moe_wd = moe_shape(moe_w_gate), moe_shape(moe_w_up), moe_shape(moe_w_down)
    ple_wp, ple_wg = ple_w_proj.astype(BF16), ple_w_gate.astype(BF16)
    src = _plain_source(xt)
    for i in range(DEPTH):
        mixer, slot = i % N_MIXERS, i // N_MIXERS
        g1, b1 = row(ln_g[i, 0]), row(ln_b[i, 0])
        if mixer == 0:
            outs = _conv_mixer(src, t, seq, slot, a_w_in_b, a_conv_w[slot], a_w_out_b, g1, b1, rwt,
                               rb)
        elif mixer == 1:
            outs = _retention_mixer(src, t, seq, slot, b_w_in.astype(BF16), b_w_out.astype(BF16),
                                    g1, b1, rwt, rb)
        else:
            outs = _sgu_mixer(src, t, slot, c_w_in.astype(BF16), row(c_ln_g[slot]),
                              row(c_ln_b[slot]), c_w_s[slot], c_b_s[slot][:, :, None],
                              c_w_out.astype(BF16), g1, b1, rwt, rb)
        x1lin, eidx, wcol = outs
        y = _moe(x1lin, eidx, wcol, i, moe_wg, moe_wu, moe_wd)
        src = _post_source(x1lin, y, i, pt, row(ln_g[i, 1]), row(ln_b[i, 1]), ple_wp, ple_wg,
                           row(ple_b_gate[i]))
    return _post(src, t).reshape(bsz, seq, d)
```

```python
import functools

import numpy as np
import jax
import jax.numpy as jnp
from jax import lax
from jax.experimental import pallas as pl
from jax.experimental.pallas import tpu as pltpu

F32 = jnp.float32
BF16 = jnp.bfloat16

D_MODEL = 1024
DEPTH = 4
N_MIXERS = 3
CONV_WIDTH = 3
RET_HEADS = 4
RET_DK = D_MODEL // RET_HEADS
RET_DV = 2 * RET_DK
RET_VDIM = RET_HEADS * RET_DV
RET_CHUNK = 256
ROPE_BASE = 10000.0
SGU_HALF = 3 * D_MODEL
SGU_GROUPS = 4
SGU_GD = SGU_HALF // SGU_GROUPS
SGU_CHUNK = 128
N_EXPERTS = 16
N_GROUPS = 4
EXPERTS_PER_GROUP = N_EXPERTS // N_GROUPS
D_EXPERT = D_MODEL // 2
PLE_DIM = 256
ALPHA = (2 * DEPTH) ** 0.25
LN_EPS = 1e-5

VMEM_LIMIT_BYTES = 56 * 1024 * 1024
SUBLANES = 8
LANES = 128
LANE_TILES = D_MODEL // LANES

TM_CONV = 1024
CONV_COLS = 1024
TM_RET = 512
TM_SGU = 512
TM_POST = 1024
TB_MOE = 2048
CH_MOE = 320
CHX_MOE = 64
EXPERTS_PER_STEP = 2
ROW_UNROLL = 64


def _const_spec(shape):
    nd = len(shape)
    return pl.BlockSpec(shape, lambda *_: (0,) * nd, pipeline_mode=pl.Buffered(1))


def _slot_spec(shape, slot):
    nd = len(shape)
    return pl.BlockSpec((1,) + tuple(shape[1:]), lambda *_: (slot,) + (0,) * (nd - 1),
                        pipeline_mode=pl.Buffered(1))


def _layer_norm(h, g, b):
    mu = jnp.mean(h, axis=-1, keepdims=True)
    hc = h - mu
    var = jnp.mean(hc * hc, axis=-1, keepdims=True)
    return hc * lax.rsqrt(var + LN_EPS) * g + b


def _top2_sum(a, b, c, d):
    m1, n1 = jnp.maximum(a, b), jnp.minimum(a, b)
    m2, n2 = jnp.maximum(c, d), jnp.minimum(c, d)
    return jnp.maximum(m1, m2) + jnp.maximum(jnp.minimum(m1, m2), jnp.maximum(n1, n2))


def _route(x1b, rwt_ref, rb_ref):
    tm = x1b.shape[0]
    logits_t = lax.dot_general(rwt_ref[...], x1b, (((1,), (1,)), ((), ())),
                               preferred_element_type=F32)
    scores = jax.nn.sigmoid(logits_t)
    sel = scores + rb_ref[...]
    s = [sel[e:e + 1, :] for e in range(N_EXPERTS)]
    sc = [scores[e:e + 1, :] for e in range(N_EXPERTS)]
    grp = [_top2_sum(*s[EXPERTS_PER_GROUP * g:EXPERTS_PER_GROUP * (g + 1)])
           for g in range(N_GROUPS)]
    best, g_idx = grp[0], jnp.zeros((1, tm), jnp.int32)
    for g in range(1, N_GROUPS):
        better = grp[g] > best
        best = jnp.where(better, grp[g], best)
        g_idx = jnp.where(better, g, g_idx)

    def pick(rows, j):
        out = rows[j]
        for g in range(1, N_GROUPS):
            out = jnp.where(g_idx == g, rows[EXPERTS_PER_GROUP * g + j], out)
        return out
    v = [pick(s, j) for j in range(EXPERTS_PER_GROUP)]
    w = [pick(sc, j) for j in range(EXPERTS_PER_GROUP)]

    def argmax4(vals):
        bv, bi = vals[0], jnp.zeros((1, tm), jnp.int32)
        for j in range(1, EXPERTS_PER_GROUP):
            better = vals[j] > bv
            bv = jnp.where(better, vals[j], bv)
            bi = jnp.where(better, j, bi)
        return bi
    i1 = argmax4(v)
    i2 = argmax4([jnp.where(i1 == j, -jnp.inf, v[j]) for j in range(EXPERTS_PER_GROUP)])

    def take(vals, idx):
        out = vals[0]
        for j in range(1, EXPERTS_PER_GROUP):
            out = jnp.where(idx == j, vals[j], out)
        return out
    w1, w2 = take(w, i1), take(w, i2)
    den = w1 + w2
    w1, w2 = w1 / den, w2 / den
    e1, e2 = g_idx * EXPERTS_PER_GROUP + i1, g_idx * EXPERTS_PER_GROUP + i2
    return e1, e2, w1, w2


def _finish_mixer(x, mix, r0, lng_ref, lnb_ref, rwt_ref, rb_ref, x1lin_ref, eidx_ref,
                  wcol_ref):
    tm = x.shape[0]
    x1 = _layer_norm(ALPHA * x + mix, lng_ref[...], lnb_ref[...])
    for c in range(LANE_TILES):
        x1lin_ref[pl.ds(r0 * LANE_TILES + c, tm, stride=LANE_TILES), :] = (
            x1[:, c * LANES:(c + 1) * LANES])
    e1, e2, w1, w2 = _route(x1.astype(BF16), rwt_ref, rb_ref)
    eidx_ref[:, r0:r0 + tm] = jnp.concatenate([e1, e2], axis=0)
    pad = jnp.zeros((LANES - 2, tm), F32)
    wcol_ref[r0:r0 + tm, :] = jnp.concatenate([w1, w2, pad], axis=0).T


def _conv_mixer_kernel(n_src, tiles_per_seq, *refs):
    src = refs[:n_src]
    (win_ref, cw_ref, wout_ref, lng_ref, lnb_ref, rwt_ref, rb_ref, x1lin_ref, eidx_ref, wcol_ref,
     carry_ref) = refs[n_src:]
    i = pl.program_id(0)
    tm = wcol_ref.shape[0]
    cg = CONV_COLS

    @pl.when(i % tiles_per_seq == 0)
    def _():
        carry_ref[...] = jnp.zeros_like(carry_ref)
    row = lax.broadcasted_iota(jnp.int32, (tm, 1), 0)
    x = _load_tile(src)
    xb = x.astype(BF16)

    def proj(lo):
        return jnp.dot(xb, win_ref[0, :, lo:lo + cg], preferred_element_type=F32)

    mix = None
    for lo in range(0, D_MODEL, cg):
        b_gate = proj(lo)
        z = proj(D_MODEL + lo) * proj(2 * D_MODEL + lo)
        prev = carry_ref[:, lo:lo + cg]
        p1, p2 = prev[SUBLANES - 1:SUBLANES], prev[SUBLANES - 2:SUBLANES - 1]
        z1 = jnp.where(row == 0, p1, pltpu.roll(z, 1, axis=0))
        z2 = jnp.where(row == 0, p2, jnp.where(row == 1, p1, pltpu.roll(z, 2, axis=0)))
        carry_ref[:, lo:lo + cg] = z[tm - SUBLANES:, :]
        cw = cw_ref[:, lo:lo + cg]
        zc = cw[0:1] * z2 + cw[1:2] * z1 + cw[2:3] * z
        part = jnp.dot((b_gate * zc).astype(BF16), wout_ref[0, lo:lo + cg, :],
                       preferred_element_type=F32)
        mix = part if mix is None else mix + part
    _finish_mixer(x, mix, 0, lng_ref, lnb_ref, rwt_ref, rb_ref, x1lin_ref, eidx_ref, wcol_ref)


def _mixer_out_shapes(t):
    return (jax.ShapeDtypeStruct((t * LANE_TILES, LANES), F32),
            jax.ShapeDtypeStruct((2, t), jnp.int32),
            jax.ShapeDtypeStruct((t, LANES), F32))


def _mixer_out_specs(tm):
    return (pl.BlockSpec((tm * LANE_TILES, LANES), lambda i: (i, 0)),
            pl.BlockSpec((2, tm), lambda i: (0, i)),
            pl.BlockSpec((tm, LANES), lambda i: (i, 0)))


def _conv_mixer(src, t, seq, slot, w_in, conv_w, w_out, ln_g, ln_b, rwt, rb):
    arrays, specs = src
    tm = TM_CONV
    assert conv_w.shape[0] == CONV_WIDTH == 3
    return pl.pallas_call(
        functools.partial(_conv_mixer_kernel, len(arrays), seq // tm),
        out_shape=_mixer_out_shapes(t),
        grid=(t // tm,),
        in_specs=[*specs(tm),
                  _slot_spec(w_in.shape, slot), _const_spec(conv_w.shape),
                  _slot_spec(w_out.shape, slot),
                  _const_spec(ln_g.shape), _const_spec(ln_b.shape),
                  _const_spec(rwt.shape), _const_spec(rb.shape)],
        out_specs=_mixer_out_specs(tm),
        scratch_shapes=[pltpu.VMEM((SUBLANES, D_MODEL), F32)],
        compiler_params=pltpu.CompilerParams(dimension_semantics=("arbitrary",),
                                             vmem_limit_bytes=VMEM_LIMIT_BYTES),
        name="conv_mixer",
    )(*arrays, w_in, conv_w, w_out, ln_g, ln_b, rwt, rb)


def _rotary(t, cos, sin):
    half = t.shape[-1] // 2
    t1, t2 = t[:, :half], t[:, half:]
    return jnp.concatenate([t1 * cos - t2 * sin, t2 * cos + t1 * sin], axis=-1)


def _retention_kernel(n_src, chunks_per_seq, *refs):
    src = refs[:n_src]
    (win_ref, wout_ref, cos_ref, sin_ref, dmask_ref, qdec_ref, kdec_ref, cdec_ref, lng_ref,
     lnb_ref, rwt_ref, rb_ref, x1lin_ref, eidx_ref, wcol_ref, state_ref) = refs[n_src:]
    i = pl.program_id(0)

    @pl.when(i % chunks_per_seq == 0)
    def _():
        state_ref[...] = jnp.zeros_like(state_ref)

    tm = wcol_ref.shape[0]
    x = _load_tile(src)
    xb = x.astype(BF16)
    cos, sin = cos_ref[...], sin_ref[...]

    def proj(lo, width):
        return jnp.dot(xb, win_ref[0, :, lo:lo + width], preferred_element_type=F32)

    mix = None
    for h in range(RET_HEADS):
        q = _rotary(proj(h * RET_DK, RET_DK), cos, sin).astype(BF16)
        k = _rotary(proj(D_MODEL + h * RET_DK, RET_DK) * (RET_DK ** -0.5), cos, sin)
        v = proj(2 * D_MODEL + h * RET_DV, RET_DV).astype(BF16)
        g = proj(2 * D_MODEL + RET_VDIM + h * RET_DV, RET_DV)
        outs = []
        for r0 in range(0, tm, RET_CHUNK):
            rows = slice(r0, r0 + RET_CHUNK)
            qb, kc, vb = q[rows], k[rows], v[rows]
            scores = lax.dot_general(qb, kc.astype(BF16), (((1,), (1,)), ((), ())),
                                     preferred_element_type=F32) * dmask_ref[h]
            inner = jnp.dot(scores.astype(BF16), vb, preferred_element_type=F32)
            state = state_ref[h]
            cross = jnp.dot(qb, state.astype(BF16), preferred_element_type=F32) * qdec_ref[h]
            kd = (kc * kdec_ref[h]).astype(BF16)
            state_ref[h] = state * cdec_ref[h] + lax.dot_general(
                kd, vb, (((0,), (0,)), ((), ())), preferred_element_type=F32)
            o = inner + cross
            mu = jnp.mean(o, axis=-1, keepdims=True)
            oc = o - mu
            outs.append(oc * lax.rsqrt(jnp.mean(oc * oc, axis=-1, keepdims=True) + LN_EPS))
        gated = (jax.nn.silu(g) * jnp.concatenate(outs, axis=0)).astype(BF16)
        part = jnp.dot(gated, wout_ref[0, h * RET_DV:(h + 1) * RET_DV, :],
                       preferred_element_type=F32)
        mix = part if mix is None else mix + part
    _finish_mixer(x, mix, 0, lng_ref, lnb_ref, rwt_ref, rb_ref, x1lin_ref, eidx_ref, wcol_ref)


def _retention_tables(seq):
    c = RET_CHUNK
    pos = np.arange(seq, dtype=np.float64)
    inv_freq = 1.0 / (ROPE_BASE ** np.linspace(0.0, 1.0, RET_DK // 2))
    ang = pos[:, None] * inv_freq[None, :]
    log_gamma = np.log(1.0 - 2.0 ** (-5.0 - np.arange(RET_HEADS, dtype=np.float64)))
    idx = np.arange(c, dtype=np.float64)
    diff = idx[:, None] - idx[None, :]
    dmask = np.where(diff >= 0, np.exp(log_gamma[:, None, None] * np.maximum(diff, 0.0)), 0.0)
    qdec = np.exp(log_gamma[:, None] * (idx[None, :] + 1.0))[:, :, None]
    kdec = np.exp(log_gamma[:, None] * (c - 1.0 - idx[None, :]))[:, :, None]
    cdec = np.broadcast_to(np.exp(log_gamma * c)[:, None, None], (RET_HEADS, 1, RET_DV))
    return tuple(jnp.asarray(a, dtype=F32)
                 for a in (np.cos(ang), np.sin(ang), dmask, qdec, kdec, cdec))


def _retention_mixer(src, t, seq, slot, w_in, w_out, ln_g, ln_b, rwt, rb):
    arrays, specs = src
    c = TM_RET
    cps = seq // c
    cos, sin, dmask, qdec, kdec, cdec = _retention_tables(seq)
    return pl.pallas_call(
        functools.partial(_retention_kernel, len(arrays), cps),
        out_shape=_mixer_out_shapes(t),
        grid=(t // c,),
        in_specs=[*specs(c),
                  _slot_spec(w_in.shape, slot), _slot_spec(w_out.shape, slot),
                  pl.BlockSpec((c, RET_DK // 2), lambda i: (i % cps, 0)),
                  pl.BlockSpec((c, RET_DK // 2), lambda i: (i % cps, 0)),
                  _const_spec(dmask.shape), _const_spec(qdec.shape), _const_spec(kdec.shape),
                  _const_spec(cdec.shape),
                  _const_spec(ln_g.shape), _const_spec(ln_b.shape),
                  _const_spec(rwt.shape), _const_spec(rb.shape)],
        out_specs=_mixer_out_specs(c),
        scratch_shapes=[pltpu.VMEM((RET_HEADS, RET_DK, RET_DV), F32)],
        compiler_params=pltpu.CompilerParams(dimension_semantics=("arbitrary",),
                                             vmem_limit_bytes=VMEM_LIMIT_BYTES),
        name="retention_mixer",
    )(*arrays, w_in, w_out, cos, sin, dmask, qdec, kdec, cdec, ln_g, ln_b, rwt, rb)


def _sgu_kernel(n_src, *refs):
    src = refs[:n_src]
    (win_ref, sg_ref, sb_ref, ws_ref, bs_ref, wout_ref, lng_ref, lnb_ref, rwt_ref, rb_ref,
     x1lin_ref, eidx_ref, wcol_ref) = refs[n_src:]
    c = SGU_CHUNK
    row = lax.broadcasted_iota(jnp.int32, (c, c), 0)
    col = lax.broadcasted_iota(jnp.int32, (c, c), 1)
    ws = [jnp.where(row >= col, ws_ref[g], 0.0).astype(BF16) for g in range(SGU_GROUPS)]

    def gelu_proj(xb, lo):
        zin = jnp.dot(xb, win_ref[0, :, lo:lo + SGU_GD], preferred_element_type=F32)
        return 0.5 * zin * (1.0 + lax.erf(zin * (2.0 ** -0.5)))

    tm = wcol_ref.shape[0]
    x = _load_tile(src)
    xb = x.astype(BF16)
    v = jnp.concatenate([gelu_proj(xb, SGU_HALF + g * SGU_GD) for g in range(SGU_GROUPS)],
                        axis=-1)
    v = _layer_norm(v, sg_ref[...], sb_ref[...]).astype(BF16)
    mix = None
    for g in range(SGU_GROUPS):
        u = gelu_proj(xb, g * SGU_GD)
        vs = jnp.concatenate(
            [jnp.dot(ws[g], v[r0:r0 + c, g * SGU_GD:(g + 1) * SGU_GD],
                     preferred_element_type=F32) + bs_ref[g] for r0 in range(0, tm, c)], axis=0)
        part = jnp.dot((u * vs).astype(BF16), wout_ref[0, g * SGU_GD:(g + 1) * SGU_GD, :],
                       preferred_element_type=F32)
        mix = part if mix is None else mix + part
    _finish_mixer(x, mix, 0, lng_ref, lnb_ref, rwt_ref, rb_ref, x1lin_ref, eidx_ref, wcol_ref)


def _sgu_mixer(src, t, slot, w_in, sgu_g, sgu_b, w_s, b_s, w_out, ln_g, ln_b, rwt, rb):
    arrays, specs = src
    c = TM_SGU
    return pl.pallas_call(
        functools.partial(_sgu_kernel, len(arrays)),
        out_shape=_mixer_out_shapes(t),
        grid=(t // c,),
        in_specs=[*specs(c),
                  _slot_spec(w_in.shape, slot), _const_spec(sgu_g.shape), _const_spec(sgu_b.shape),
                  _const_spec(w_s.shape), _const_spec(b_s.shape), _slot_spec(w_out.shape, slot),
                  _const_spec(ln_g.shape), _const_spec(ln_b.shape),
                  _const_spec(rwt.shape), _const_spec(rb.shape)],
        out_specs=_mixer_out_specs(c),
        compiler_params=pltpu.CompilerParams(dimension_semantics=("arbitrary",),
                                             vmem_limit_bytes=VMEM_LIMIT_BYTES),
        name="sgu_mixer",
    )(*arrays, w_in, sgu_g, sgu_b, w_s, b_s, w_out, ln_g, ln_b, rwt, rb)


PLAN_BL = 256


def _plan_kernel(eidx_ref, dest_ref, offs_ref):
    tb = eidx_ref.shape[1]
    ns = 2 * tb
    e_all = jnp.concatenate([eidx_ref[0:1, :], eidx_ref[1:2, :]], axis=1)
    eid = lax.broadcasted_iota(jnp.int32, (N_EXPERTS, ns), 0)
    mask = eid == e_all
    off = jnp.sum(jnp.where(e_all < eid, 1.0, 0.0), axis=1, keepdims=True)
    r = lax.broadcasted_iota(jnp.int32, (PLAN_BL, PLAN_BL), 0)
    c = lax.broadcasted_iota(jnp.int32, (PLAN_BL, PLAN_BL), 1)
    upper = jnp.where(r < c, 1.0, 0.0).astype(BF16)
    carry = off
    pieces = []
    for b in range(ns // PLAN_BL):
        mb = mask[:, b * PLAN_BL:(b + 1) * PLAN_BL]
        mbf = jnp.where(mb, 1.0, 0.0)
        pos = jnp.dot(mbf.astype(BF16), upper, preferred_element_type=F32) + carry
        pieces.append(jnp.sum(jnp.where(mb, pos, 0.0), axis=0, keepdims=True))
        carry = carry + jnp.sum(mbf, axis=1, keepdims=True)
    dest_ref[0] = jnp.concatenate(pieces, axis=1).astype(jnp.int32)
    sub = lax.broadcasted_iota(jnp.int32, (N_EXPERTS, LANES), 0)
    lane = lax.broadcasted_iota(jnp.int32, (N_EXPERTS, LANES), 1)
    offs_row = jnp.sum(jnp.where(sub == lane, off, 0.0), axis=0, keepdims=True)
    lane1 = lax.broadcasted_iota(jnp.int32, (1, LANES), 1)
    offs_ref[0] = jnp.where(lane1 == N_EXPERTS, float(ns), offs_row).astype(jnp.int32)


def _plan(eidx, tb):
    t = eidx.shape[1]
    nb = t // tb
    return pl.pallas_call(
        _plan_kernel,
        out_shape=(jax.ShapeDtypeStruct((nb, 1, 2 * tb), jnp.int32),
                   jax.ShapeDtypeStruct((nb, 1, LANES), jnp.int32)),
        grid=(nb,),
        in_specs=[pl.BlockSpec((2, tb), lambda i: (0, i))],
        out_specs=(pl.BlockSpec((1, 1, 2 * tb), lambda i: (i, 0, 0)),
                   pl.BlockSpec((1, 1, LANES), lambda i: (i, 0, 0))),
        compiler_params=pltpu.CompilerParams(dimension_semantics=("arbitrary",)),
        name="moe_plan",
    )(eidx)


def _expert_ffn(g_ref, o_ref, q, wg_ref, wu_ref, wd_ref):
    ch = g_ref.shape[0] // LANE_TILES
    xg = jnp.concatenate([g_ref[pl.ds(cc, ch, stride=LANE_TILES), :]
                          for cc in range(LANE_TILES)], axis=1).astype(BF16)
    hg = jnp.dot(xg, wg_ref[q], preferred_element_type=F32)
    hu = jnp.dot(xg, wu_ref[q], preferred_element_type=F32)
    he = (jax.nn.silu(hg) * hu).astype(BF16)
    o = jnp.dot(he, wd_ref[q], preferred_element_type=F32)
    for cc in range(LANE_TILES):
        o_ref[pl.ds(cc, ch, stride=LANE_TILES), :] = o[:, cc * LANES:(cc + 1) * LANES]


def _moe_kernel(dest_ref, offs_ref, z_ref, wcol_ref, wg_ref, wu_ref, wd_ref, y_ref,
                inv_ref, g_ref, o_ref, gx_ref, ox_ref, y2_ref):
    i = pl.program_id(0)
    k = pl.program_id(1)
    eps = wg_ref.shape[0]
    e0 = k * eps
    tb = y_ref.shape[0]
    ns = 2 * tb
    ch = g_ref.shape[1] // LANE_TILES
    chx = gx_ref.shape[0] // LANE_TILES
    tok_mask = tb * LANE_TILES - 1
    rows_at = lambda off: pl.ds(pl.multiple_of(off, LANE_TILES), LANE_TILES)

    @pl.when((i == 0) & (k == 0))
    def _():
        o_ref[...] = jnp.zeros_like(o_ref)

    @pl.when(k == 0)
    def _():
        def pad_body(r, carry):
            inv_ref[ns + r] = ns * LANE_TILES
            return carry
        lax.fori_loop(0, ch, pad_body, 0)

        def inv_body(r, carry):
            for u in range(ROW_UNROLL):
                s = r * ROW_UNROLL + u
                inv_ref[dest_ref[0, 0, s]] = s * LANE_TILES
            return carry
        lax.fori_loop(0, ns // ROW_UNROLL, inv_body, 0)

        def gather_body(r, carry):
            for u in range(ROW_UNROLL):
                j = r * ROW_UNROLL + u
                g_ref[0, rows_at(j * LANE_TILES), :] = z_ref[rows_at(inv_ref[j] & tok_mask), :]
            return carry
        lax.fori_loop(0, ch // ROW_UNROLL, gather_body, 0)

    bounds = [offs_ref[0, 0, e0 + q] for q in range(eps + 1)]
    prev_start = jnp.where(k == 0, ns, offs_ref[0, 0, jnp.maximum(e0 - 1, 0)])

    for q in range(eps):
        cur, nxt = q % 2, 1 - q % 2
        scatter_base = prev_start if q == 0 else bounds[q - 1]
        for j in range(ch):
            y2_ref[rows_at(inv_ref[scatter_base + j]), :] = (
                o_ref[nxt, j * LANE_TILES:(j + 1) * LANE_TILES, :])
        _expert_ffn(g_ref.at[cur], o_ref.at[cur], q, wg_ref, wu_ref, wd_ref)
        for j in range(ch):
            g_ref[nxt, j * LANE_TILES:(j + 1) * LANE_TILES, :] = (
                z_ref[rows_at(inv_ref[bounds[q + 1] + j] & tok_mask), :])

    for q in range(eps):
        start, end = bounds[q], bounds[q + 1]

        def extra_body(c, carry, start=start, q=q):
            a = start + ch + c * chx

            def gather_body(r, carry2):
                for u in range(SUBLANES):
                    j = r * SUBLANES + u
                    gx_ref[rows_at(j * LANE_TILES), :] = (
                        z_ref[rows_at(inv_ref[a + j] & tok_mask), :])
                return carry2
            lax.fori_loop(0, chx // SUBLANES, gather_body, 0)
            _expert_ffn(gx_ref, ox_ref, q, wg_ref, wu_ref, wd_ref)

            def scatter_body(r, carry2):
                for u in range(SUBLANES):
                    j = r * SUBLANES + u
                    y2_ref[rows_at(inv_ref[a + j]), :] = ox_ref[rows_at(j * LANE_TILES), :]
                return carry2
            lax.fori_loop(0, chx // SUBLANES, scatter_body, 0)
            return carry
        lax.fori_loop(0, (jnp.maximum(end - start - ch, 0) + (chx - 1)) // chx, extra_body, 0)

    @pl.when(k == pl.num_programs(1) - 1)
    def _():
        last = (eps - 1) % 2

        def scatter_body(r, carry):
            for u in range(ROW_UNROLL):
                j = r * ROW_UNROLL + u
                y2_ref[rows_at(inv_ref[bounds[eps - 1] + j]), :] = (
                    o_ref[last, rows_at(j * LANE_TILES), :])
            return carry
        lax.fori_loop(0, ch // ROW_UNROLL, scatter_body, 0)
        w = wcol_ref[...]
        w1 = jnp.broadcast_to(w[:, 0:1], (tb, LANES))
        w2 = jnp.broadcast_to(w[:, 1:2], (tb, LANES))
        for cc in range(LANE_TILES):
            y_first = y2_ref[pl.ds(cc, tb, stride=LANE_TILES), :]
            y_second = y2_ref[pl.ds(tb * LANE_TILES + cc, tb, stride=LANE_TILES), :]
            y_ref[:, cc * LANES:(cc + 1) * LANES] = (
                w1 * y_first + w2 * y_second).astype(y_ref.dtype)


def _moe(x1lin, eidx, wcol, layer, w_gate, w_up, w_down):
    t = wcol.shape[0]
    tb, ch = TB_MOE, CH_MOE
    dest, offs = _plan(eidx, tb)
    eps = EXPERTS_PER_STEP
    assert eps % 2 == 0 and N_EXPERTS % eps == 0
    steps = N_EXPERTS // eps
    smem_spec = lambda n: pl.BlockSpec((1, 1, n), lambda i, e: (i, 0, 0), memory_space=pltpu.SMEM)
    expert_block = lambda i, k: (layer * steps + k, 0, 0)
    return pl.pallas_call(
        _moe_kernel,
        out_shape=jax.ShapeDtypeStruct((t, D_MODEL), BF16),
        grid=(t // tb, steps),
        in_specs=[smem_spec(2 * tb), smem_spec(LANES),
                  pl.BlockSpec((tb * LANE_TILES, LANES), lambda i, e: (i, 0),
                               pipeline_mode=pl.Buffered(1)),
                  pl.BlockSpec((tb, LANES), lambda i, e: (i, 0)),
                  pl.BlockSpec((eps, D_MODEL, D_EXPERT), expert_block),
                  pl.BlockSpec((eps, D_MODEL, D_EXPERT), expert_block),
                  pl.BlockSpec((eps, D_EXPERT, D_MODEL), expert_block)],
        out_specs=pl.BlockSpec((tb, D_MODEL), lambda i, e: (i, 0), pipeline_mode=pl.Buffered(1)),
        scratch_shapes=[pltpu.SMEM((2 * tb + ch,), jnp.int32),
                        pltpu.VMEM((2, ch * LANE_TILES, LANES), F32),
                        pltpu.VMEM((2, ch * LANE_TILES, LANES), F32),
                        pltpu.VMEM((CHX_MOE * LANE_TILES, LANES), F32),
                        pltpu.VMEM((CHX_MOE * LANE_TILES, LANES), F32),
                        pltpu.VMEM(((2 * tb + 1) * LANE_TILES, LANES), F32)],
        compiler_params=pltpu.CompilerParams(dimension_semantics=("arbitrary", "arbitrary"),
                                             vmem_limit_bytes=VMEM_LIMIT_BYTES),
        name="moe",
    )(dest, offs, x1lin, wcol, w_gate, w_up, w_down)


def _post_tile(x1lin_ref, y_ref, p_ref, lng_ref, lnb_ref, wp_ref, wg_ref, bg_ref):
    tm = y_ref.shape[0]
    x1 = jnp.concatenate([x1lin_ref[pl.ds(c, tm, stride=LANE_TILES), :]
                          for c in range(LANE_TILES)], axis=1)
    x2 = _layer_norm(ALPHA * x1 + y_ref[...].astype(F32), lng_ref[...], lnb_ref[...])
    emb = jnp.dot(p_ref[0].astype(BF16), wp_ref[0], preferred_element_type=F32)
    gate = jax.nn.sigmoid(
        jnp.dot(x2.astype(BF16), wg_ref[0], preferred_element_type=F32) + bg_ref[...])
    return x2 + gate * emb


def _load_tile(src):
    return src[0][...] if len(src) == 1 else _post_tile(*src)


def _plain_source(x):
    return (x,), lambda tm: [pl.BlockSpec((tm, D_MODEL), lambda i: (i, 0))]


def _post_source(x1lin, y, layer, p, ln_g, ln_b, w_proj, w_gate, b_gate):
    specs = lambda tm: [pl.BlockSpec((tm * LANE_TILES, LANES), lambda i: (i, 0)),
                        pl.BlockSpec((tm, D_MODEL), lambda i: (i, 0)),
                        pl.BlockSpec((1, tm, PLE_DIM), lambda i: (layer, i, 0)),
                        _const_spec(ln_g.shape), _const_spec(ln_b.shape),
                        _slot_spec(w_proj.shape, layer), _slot_spec(w_gate.shape, layer),
                        _const_spec(b_gate.shape)]
    return (x1lin, y, p, ln_g, ln_b, w_proj, w_gate, b_gate), specs


def _post_kernel(*refs):
    refs[-1][...] = _post_tile(*refs[:-1])


def _post(src, t):
    arrays, specs = src
    tm = TM_POST
    return pl.pallas_call(
        _post_kernel,
        out_shape=jax.ShapeDtypeStruct((t, D_MODEL), F32),
        grid=(t // tm,),
        in_specs=specs(tm),
        out_specs=pl.BlockSpec((tm, D_MODEL), lambda i: (i, 0)),
        compiler_params=pltpu.CompilerParams(dimension_semantics=("arbitrary",),
                                             vmem_limit_bytes=VMEM_LIMIT_BYTES),
        name="post_moe",
    )(*arrays)


def kernel(x, p, a_w_in, a_conv_w, a_w_out, b_w_in, b_w_out, c_w_in, c_ln_g, c_ln_b, c_w_s, c_b_s,
           c_w_out, router_w, router_b, moe_w_gate, moe_w_up, moe_w_down, ln_g, ln_b, ple_w_proj,
           ple_w_gate, ple_b_gate):
    bsz, seq, d = x.shape
    t = bsz * seq
    xt = x.reshape(t, d)
    pt = p.reshape(DEPTH, t, PLE_DIM)
    rwt = router_w.T.astype(BF16)
    rb = router_b.astype(F32).reshape(N_EXPERTS, 1)
    row = lambda a: a.reshape(1, -1)
    a_w_in_b, a_w_out_b = a_w_in.astype(BF16), a_w_out.astype(BF16)
    moe_shape = lambda w: w.astype(BF16).reshape((DEPTH * N_EXPERTS,) + w.shape[2:])
    moe_wg, moe_wu, moe_wd = moe_shape(moe_w_gate), moe_shape(moe_w_up), moe_shape(moe_w_down)
    ple_wp, ple_wg = ple_w_proj.astype(BF16), ple_w_gate.astype(BF16)
    src = _plain_source(xt)
    for i in range(DEPTH):
        mixer, slot = i % N_MIXERS, i // N_MIXERS
        g1, b1 = row(ln_g[i, 0]), row(ln_b[i, 0])
        if mixer == 0:
            outs = _conv_mixer(src, t, seq, slot, a_w_in_b, a_conv_w[slot], a_w_out_b, g1, b1, rwt,
                               rb)
        elif mixer == 1:
            outs = _retention_mixer(src, t, seq, slot, b_w_in.astype(BF16), b_w_out.astype(BF16),
                                    g1, b1, rwt, rb)
        else:
            outs = _sgu_mixer(src, t, slot, c_w_in.astype(BF16), row(c_ln_g[slot]),
                              row(c_ln_b[slot]), c_w_s[slot], c_b_s[slot][:, :, None],
                              c_w_out.astype(BF16), g1, b1, rwt, rb)
        x1lin, eidx, wcol = outs
        y = _moe(x1lin, eidx, wcol, i, moe_wg, moe_wu, moe_wd)
        src = _post_source(x1lin, y, i, pt, row(ln_g[i, 1]), row(ln_b[i, 1]), ple_wp, ple_wg,
                           row(ple_b_gate[i]))
    return _post(src, t).reshape(bsz, seq, d)
```

```python
import functools

import numpy as np
import jax
import jax.numpy as jnp
from jax import lax
from jax.experimental import pallas as pl
from jax.experimental.pallas import tpu as pltpu

F32 = jnp.float32
BF16 = jnp.bfloat16

D_MODEL = 1024
DEPTH = 4
N_MIXERS = 3
CONV_WIDTH = 3
RET_HEADS = 4
RET_DK = D_MODEL // RET_HEADS
RET_DV = 2 * RET_DK
RET_VDIM = RET_HEADS * RET_DV
RET_CHUNK = 256
ROPE_BASE = 10000.0
SGU_HALF = 3 * D_MODEL
SGU_GROUPS = 4
SGU_GD = SGU_HALF // SGU_GROUPS
SGU_CHUNK = 128
N_EXPERTS = 16
N_GROUPS = 4
EXPERTS_PER_GROUP = N_EXPERTS // N_GROUPS
D_EXPERT = D_MODEL // 2
PLE_DIM = 256
ALPHA = (2 * DEPTH) ** 0.25
LN_EPS = 1e-5

VMEM_LIMIT_BYTES = 56 * 1024 * 1024
SUBLANES = 8
LANES = 128
LANE_TILES = D_MODEL // LANES

TM_CONV = 1024
CONV_COLS = 1024
TM_RET = 512
TM_SGU = 512
TM_POST = 1024
TB_MOE = 2048
CH_MOE = 320
CHX_MOE = 64
EXPERTS_PER_STEP = 2
ROW_UNROLL = 64


def _const_spec(shape):
    nd = len(shape)
    return pl.BlockSpec(shape, lambda *_: (0,) * nd, pipeline_mode=pl.Buffered(1))


def _slot_spec(shape, slot):
    nd = len(shape)
    return pl.BlockSpec((1,) + tuple(shape[1:]), lambda *_: (slot,) + (0,) * (nd - 1),
                        pipeline_mode=pl.Buffered(1))


def _layer_norm(h, g, b):
    mu = jnp.mean(h, axis=-1, keepdims=True)
    hc = h - mu
    var = jnp.mean(hc * hc, axis=-1, keepdims=True)
    return hc * lax.rsqrt(var + LN_EPS) * g + b


def _top2_sum(a, b, c, d):
    m1, n1 = jnp.maximum(a, b), jnp.minimum(a, b)
    m2, n2 = jnp.maximum(c, d), jnp.minimum(c, d)
    return jnp.maximum(m1, m2) + jnp.maximum(jnp.minimum(m1, m2), jnp.maximum(n1, n2))


def _route(x1b, rwt_ref, rb_ref):
    tm = x1b.shape[0]
    logits_t = lax.dot_general(rwt_ref[...], x1b, (((1,), (1,)), ((), ())),
                               preferred_element_type=F32)
    scores = jax.nn.sigmoid(logits_t)
    sel = scores + rb_ref[...]
    s = [sel[e:e + 1, :] for e in range(N_EXPERTS)]
    sc = [scores[e:e + 1, :] for e in range(N_EXPERTS)]
    grp = [_top2_sum(*s[EXPERTS_PER_GROUP * g:EXPERTS_PER_GROUP * (g + 1)])
           for g in range(N_GROUPS)]
    best, g_idx = grp[0], jnp.zeros((1, tm), jnp.int32)
    for g in range(1, N_GROUPS):
        better = grp[g] > best
        best = jnp.where(better, grp[g], best)
        g_idx = jnp.where(better, g, g_idx)

    def pick(rows, j):
        out = rows[j]
        for g in range(1, N_GROUPS):
            out = jnp.where(g_idx == g, rows[EXPERTS_PER_GROUP * g + j], out)
        return out
    v = [pick(s, j) for j in range(EXPERTS_PER_GROUP)]
    w = [pick(sc, j) for j in range(EXPERTS_PER_GROUP)]

    def argmax4(vals):
        bv, bi = vals[0], jnp.zeros((1, tm), jnp.int32)
        for j in range(1, EXPERTS_PER_GROUP):
            better = vals[j] > bv
            bv = jnp.where(better, vals[j], bv)
            bi = jnp.where(better, j, bi)
        return bi
    i1 = argmax4(v)
    i2 = argmax4([jnp.where(i1 == j, -jnp.inf, v[j]) for j in range(EXPERTS_PER_GROUP)])

    def take(vals, idx):
        out = vals[0]
        for j in range(1, EXPERTS_PER_GROUP):
            out = jnp.where(idx == j, vals[j], out)
        return out
    w1, w2 = take(w, i1), take(w, i2)
    den = w1 + w2
    w1, w2 = w1 / den, w2 / den
    e1, e2 = g_idx * EXPERTS_PER_GROUP + i1, g_idx * EXPERTS_PER_GROUP + i2
    return e1, e2, w1, w2


def _finish_mixer(x, mix, r0, lng_ref, lnb_ref, rwt_ref, rb_ref, x1lin_ref, eidx_ref,
                  wcol_ref):
    tm = x.shape[0]
    x1 = _layer_norm(ALPHA * x + mix, lng_ref[...], lnb_ref[...])
    for c in range(LANE_TILES):
        x1lin_ref[pl.ds(r0 * LANE_TILES + c, tm, stride=LANE_TILES), :] = (
            x1[:, c * LANES:(c + 1) * LANES])
    e1, e2, w1, w2 = _route(x1.astype(BF16), rwt_ref, rb_ref)
    eidx_ref[:, r0:r0 + tm] = jnp.concatenate([e1, e2], axis=0)
    pad = jnp.zeros((LANES - 2, tm), F32)
    wcol_ref[r0:r0 + tm, :] = jnp.concatenate([w1, w2, pad], axis=0).T


def _conv_mixer_kernel(n_src, tiles_per_seq, *refs):
    src = refs[:n_src]
    (win_ref, cw_ref, wout_ref, lng_ref, lnb_ref, rwt_ref, rb_ref, x1lin_ref, eidx_ref, wcol_ref,
     carry_ref) = refs[n_src:]
    i = pl.program_id(0)
    tm = wcol_ref.shape[0]
    cg = CONV_COLS

    @pl.when(i % tiles_per_seq == 0)
    def _():
        carry_ref[...] = jnp.zeros_like(carry_ref)
    row = lax.broadcasted_iota(jnp.int32, (tm, 1), 0)
    x = _load_tile(src)
    xb = x.astype(BF16)

    def proj(lo):
        return jnp.dot(xb, win_ref[0, :, lo:lo + cg], preferred_element_type=F32)

    mix = None
    for lo in range(0, D_MODEL, cg):
        b_gate = proj(lo)
        z = proj(D_MODEL + lo) * proj(2 * D_MODEL + lo)
        prev = carry_ref[:, lo:lo + cg]
        p1, p2 = prev[SUBLANES - 1:SUBLANES], prev[SUBLANES - 2:SUBLANES - 1]
        z1 = jnp.where(row == 0, p1, pltpu.roll(z, 1, axis=0))
        z2 = jnp.where(row == 0, p2, jnp.where(row == 1, p1, pltpu.roll(z, 2, axis=0)))
        carry_ref[:, lo:lo + cg] = z[tm - SUBLANES:, :]
        cw = cw_ref[:, lo:lo + cg]
        zc = cw[0:1] * z2 + cw[1:2] * z1 + cw[2:3] * z
        part = jnp.dot((b_gate * zc).astype(BF16), wout_ref[0, lo:lo + cg, :],
                       preferred_element_type=F32)
        mix = part if mix is None else mix + part
    _finish_mixer(x, mix, 0, lng_ref, lnb_ref, rwt_ref, rb_ref, x1lin_ref, eidx_ref, wcol_ref)


def _mixer_out_shapes(t):
    return (jax.ShapeDtypeStruct((t * LANE_TILES, LANES), F32),
            jax.ShapeDtypeStruct((2, t), jnp.int32),
            jax.ShapeDtypeStruct((t, LANES), F32))


def _mixer_out_specs(tm):
    return (pl.BlockSpec((tm * LANE_TILES, LANES), lambda i: (i, 0)),
            pl.BlockSpec((2, tm), lambda i: (0, i)),
            pl.BlockSpec((tm, LANES), lambda i: (i, 0)))


def _conv_mixer(src, t, seq, slot, w_in, conv_w, w_out, ln_g, ln_b, rwt, rb):
    arrays, specs = src
    tm = TM_CONV
    assert conv_w.shape[0] == CONV_WIDTH == 3
    return pl.pallas_call(
        functools.partial(_conv_mixer_kernel, len(arrays), seq // tm),
        out_shape=_mixer_out_shapes(t),
        grid=(t // tm,),
        in_specs=[*specs(tm),
                  _slot_spec(w_in.shape, slot), _const_spec(conv_w.shape),
                  _slot_spec(w_out.shape, slot),
                  _const_spec(ln_g.shape), _const_spec(ln_b.shape),
                  _const_spec(rwt.shape), _const_spec(rb.shape)],
        out_specs=_mixer_out_specs(tm),
        scratch_shapes=[pltpu.VMEM((SUBLANES, D_MODEL), F32)],
        compiler_params=pltpu.CompilerParams(dimension_semantics=("arbitrary",),
                                             vmem_limit_bytes=VMEM_LIMIT_BYTES),
        name="conv_mixer",
    )(*arrays, w_in, conv_w, w_out, ln_g, ln_b, rwt, rb)


def _rotary(t, cos, sin):
    half = t.shape[-1] // 2
    t1, t2 = t[:, :half], t[:, half:]
    return jnp.concatenate([t1 * cos - t2 * sin, t2 * cos + t1 * sin], axis=-1)


def _retention_kernel(n_src, chunks_per_seq, *refs):
    src = refs[:n_src]
    (win_ref, wout_ref, cos_ref, sin_ref, dmask_ref, qdec_ref, kdec_ref, cdec_ref, lng_ref,
     lnb_ref, rwt_ref, rb_ref, x1lin_ref, eidx_ref, wcol_ref, state_ref) = refs[n_src:]
    i = pl.program_id(0)

    @pl.when(i % chunks_per_seq == 0)
    def _():
        state_ref[...] = jnp.zeros_like(state_ref)

    tm = wcol_ref.shape[0]
    x = _load_tile(src)
    xb = x.astype(BF16)
    cos, sin = cos_ref[...], sin_ref[...]

    def proj(lo, width):
        return jnp.dot(xb, win_ref[0, :, lo:lo + width], preferred_element_type=F32)

    mix = None
    for h in range(RET_HEADS):
        q = _rotary(proj(h * RET_DK, RET_DK), cos, sin).astype(BF16)
        k = _rotary(proj(D_MODEL + h * RET_DK, RET_DK) * (RET_DK ** -0.5), cos, sin)
        v = proj(2 * D_MODEL + h * RET_DV, RET_DV).astype(BF16)
        g = proj(2 * D_MODEL + RET_VDIM + h * RET_DV, RET_DV)
        outs = []
        for r0 in range(0, tm, RET_CHUNK):
            rows = slice(r0, r0 + RET_CHUNK)
            qb, kc, vb = q[rows], k[rows], v[rows]
            scores = lax.dot_general(qb, kc.astype(BF16), (((1,), (1,)), ((), ())),
                                     preferred_element_type=F32) * dmask_ref[h]
            inner = jnp.dot(scores.astype(BF16), vb, preferred_element_type=F32)
            state = state_ref[h]
            cross = jnp.dot(qb, state.astype(BF16), preferred_element_type=F32) * qdec_ref[h]
            kd = (kc * kdec_ref[h]).astype(BF16)
            state_ref[h] = state * cdec_ref[h] + lax.dot_general(
                kd, vb, (((0,), (0,)), ((), ())), preferred_element_type=F32)
            o = inner + cross
            mu = jnp.mean(o, axis=-1, keepdims=True)
            oc = o - mu
            outs.append(oc * lax.rsqrt(jnp.mean(oc * oc, axis=-1, keepdims=True) + LN_EPS))
        gated = (jax.nn.silu(g) * jnp.concatenate(outs, axis=0)).astype(BF16)
        part = jnp.dot(gated, wout_ref[0, h * RET_DV:(h + 1) * RET_DV, :],
                       preferred_element_type=F32)
        mix = part if mix is None else mix + part
    _finish_mixer(x, mix, 0, lng_ref, lnb_ref, rwt_ref, rb_ref, x1lin_ref, eidx_ref, wcol_ref)


def _retention_tables(seq):
    c = RET_CHUNK
    pos = np.arange(seq, dtype=np.float64)
    inv_freq = 1.0 / (ROPE_BASE ** np.linspace(0.0, 1.0, RET_DK // 2))
    ang = pos[:, None] * inv_freq[None, :]
    log_gamma = np.log(1.0 - 2.0 ** (-5.0 - np.arange(RET_HEADS, dtype=np.float64)))
    idx = np.arange(c, dtype=np.float64)
    diff = idx[:, None] - idx[None, :]
    dmask = np.where(diff >= 0, np.exp(log_gamma[:, None, None] * np.maximum(diff, 0.0)), 0.0)
    qdec = np.exp(log_gamma[:, None] * (idx[None, :] + 1.0))[:, :, None]
    kdec = np.exp(log_gamma[:, None] * (c - 1.0 - idx[None, :]))[:, :, None]
    cdec = np.broadcast_to(np.exp(log_gamma * c)[:, None, None], (RET_HEADS, 1, RET_DV))
    return tuple(jnp.asarray(a, dtype=F32)
                 for a in (np.cos(ang), np.sin(ang), dmask, qdec, kdec, cdec))


def _retention_mixer(src, t, seq, slot, w_in, w_out, ln_g, ln_b, rwt, rb):
    arrays, specs = src
    c = TM_RET
    cps = seq // c
    cos, sin, dmask, qdec, kdec, cdec = _retention_tables(seq)
    return pl.pallas_call(
        functools.partial(_retention_kernel, len(arrays), cps),
        out_shape=_mixer_out_shapes(t),
        grid=(t // c,),
        in_specs=[*specs(c),
                  _slot_spec(w_in.shape, slot), _slot_spec(w_out.shape, slot),
                  pl.BlockSpec((c, RET_DK // 2), lambda i: (i % cps, 0)),
                  pl.BlockSpec((c, RET_DK // 2), lambda i: (i % cps, 0)),
                  _const_spec(dmask.shape), _const_spec(qdec.shape), _const_spec(kdec.shape),
                  _const_spec(cdec.shape),
                  _const_spec(ln_g.shape), _const_spec(ln_b.shape),
                  _const_spec(rwt.shape), _const_spec(rb.shape)],
        out_specs=_mixer_out_specs(c),
        scratch_shapes=[pltpu.VMEM((RET_HEADS, RET_DK, RET_DV), F32)],
        compiler_params=pltpu.CompilerParams(dimension_semantics=("arbitrary",),
                                             vmem_limit_bytes=VMEM_LIMIT_BYTES),
        name="retention_mixer",
    )(*arrays, w_in, w_out, cos, sin, dmask, qdec, kdec, cdec, ln_g, ln_b, rwt, rb)


def _sgu_kernel(n_src, *refs):
    src = refs[:n_src]
    (win_ref, sg_ref, sb_ref, ws_ref, bs_ref, wout_ref, lng_ref, lnb_ref, rwt_ref, rb_ref,
     x1lin_ref, eidx_ref, wcol_ref) = refs[n_src:]
    c = SGU_CHUNK
    row = lax.broadcasted_iota(jnp.int32, (c, c), 0)
    col = lax.broadcasted_iota(jnp.int32, (c, c), 1)
    ws = [jnp.where(row >= col, ws_ref[g], 0.0).astype(BF16) for g in range(SGU_GROUPS)]

    def gelu_proj(xb, lo):
        zin = jnp.dot(xb, win_ref[0, :, lo:lo + SGU_GD], preferred_element_type=F32)
        return 0.5 * zin * (1.0 + lax.erf(zin * (2.0 ** -0.5)))

    tm = wcol_ref.shape[0]
    x = _load_tile(src)
    xb = x.astype(BF16)
    v = jnp.concatenate([gelu_proj(xb, SGU_HALF + g * SGU_GD) for g in range(SGU_GROUPS)],
                        axis=-1)
    v = _layer_norm(v, sg_ref[...], sb_ref[...]).astype(BF16)
    mix = None
    for g in range(SGU_GROUPS):
        u = gelu_proj(xb, g * SGU_GD)
        vs = jnp.concatenate(
            [jnp.dot(ws[g], v[r0:r0 + c, g * SGU_GD:(g + 1) * SGU_GD],
                     preferred_element_type=F32) + bs_ref[g] for r0 in range(0, tm, c)], axis=0)
        part = jnp.dot((u * vs).astype(BF16), wout_ref[0, g * SGU_GD:(g + 1) * SGU_GD, :],
                       preferred_element_type=F32)
        mix = part if mix is None else mix + part
    _finish_mixer(x, mix, 0, lng_ref, lnb_ref, rwt_ref, rb_ref, x1lin_ref, eidx_ref, wcol_ref)


def _sgu_mixer(src, t, slot, w_in, sgu_g, sgu_b, w_s, b_s, w_out, ln_g, ln_b, rwt, rb):
    arrays, specs = src
    c = TM_SGU
    return pl.pallas_call(
        functools.partial(_sgu_kernel, len(arrays)),
        out_shape=_mixer_out_shapes(t),
        grid=(t // c,),
        in_specs=[*specs(c),
                  _slot_spec(w_in.shape, slot), _const_spec(sgu_g.shape), _const_spec(sgu_b.shape),
                  _const_spec(w_s.shape), _const_spec(b_s.shape), _slot_spec(w_out.shape, slot),
                  _const_spec(ln_g.shape), _const_spec(ln_b.shape),
                  _const_spec(rwt.shape), _const_spec(rb.shape)],
        out_specs=_mixer_out_specs(c),
        compiler_params=pltpu.CompilerParams(dimension_semantics=("arbitrary",),
                                             vmem_limit_bytes=VMEM_LIMIT_BYTES),
        name="sgu_mixer",
    )(*arrays, w_in, sgu_g, sgu_b, w_s, b_s, w_out, ln_g, ln_b, rwt, rb)


PLAN_BL = 256


def _plan_kernel(eidx_ref, dest_ref, offs_ref):
    tb = eidx_ref.shape[1]
    ns = 2 * tb
    e_all = jnp.concatenate([eidx_ref[0:1, :], eidx_ref[1:2, :]], axis=1)
    eid = lax.broadcasted_iota(jnp.int32, (N_EXPERTS, ns), 0)
    mask = eid == e_all
    off = jnp.sum(jnp.where(e_all < eid, 1.0, 0.0), axis=1, keepdims=True)
    r = lax.broadcasted_iota(jnp.int32, (PLAN_BL, PLAN_BL), 0)
    c = lax.broadcasted_iota(jnp.int32, (PLAN_BL, PLAN_BL), 1)
    upper = jnp.where(r < c, 1.0, 0.0).astype(BF16)
    carry = off
    pieces = []
    for b in range(ns // PLAN_BL):
        mb = mask[:, b * PLAN_BL:(b + 1) * PLAN_BL]
        mbf = jnp.where(mb, 1.0, 0.0)
        pos = jnp.dot(mbf.astype(BF16), upper, preferred_element_type=F32) + carry
        pieces.append(jnp.sum(jnp.where(mb, pos, 0.0), axis=0, keepdims=True))
        carry = carry + jnp.sum(mbf, axis=1, keepdims=True)
    dest_ref[0] = jnp.concatenate(pieces, axis=1).astype(jnp.int32)
    sub = lax.broadcasted_iota(jnp.int32, (N_EXPERTS, LANES), 0)
    lane = lax.broadcasted_iota(jnp.int32, (N_EXPERTS, LANES), 1)
    offs_row = jnp.sum(jnp.where(sub == lane, off, 0.0), axis=0, keepdims=True)
    lane1 = lax.broadcasted_iota(jnp.int32, (1, LANES), 1)
    offs_ref[0] = jnp.where(lane1 == N_EXPERTS, float(ns), offs_row).astype(jnp.int32)


def _plan(eidx, tb):
    t = eidx.shape[1]
    nb = t // tb
    return pl.pallas_call(
        _plan_kernel,
        out_shape=(jax.ShapeDtypeStruct((nb, 1, 2 * tb), jnp.int32),
                   jax.ShapeDtypeStruct((nb, 1, LANES), jnp.int32)),
        grid=(nb,),
        in_specs=[pl.BlockSpec((2, tb), lambda i: (0, i))],
        out_specs=(pl.BlockSpec((1, 1, 2 * tb), lambda i: (i, 0, 0)),
                   pl.BlockSpec((1, 1, LANES), lambda i: (i, 0, 0))),
        compiler_params=pltpu.CompilerParams(dimension_semantics=("arbitrary",)),
        name="moe_plan",
    )(eidx)


CAST_EXPERTS = 2


def _cast_gate_up_kernel(g_ref, u_ref, o_ref):
    for q in range(CAST_EXPERTS):
        o_ref[q, :, :D_EXPERT] = g_ref[0, q].astype(BF16)
        o_ref[q, :, D_EXPERT:] = u_ref[0, q].astype(BF16)


def _cast_gate_up(w_gate, w_up):
    depth, ne = w_gate.shape[:2]
    ce = CAST_EXPERTS
    steps = ne // ce
    in_spec = pl.BlockSpec((1, ce, D_MODEL, D_EXPERT), lambda i: (i // steps, i % steps, 0, 0))
    return pl.pallas_call(
        _cast_gate_up_kernel,
        out_shape=jax.ShapeDtypeStruct((depth * ne, D_MODEL, 2 * D_EXPERT), BF16),
        grid=(depth * steps,),
        in_specs=[in_spec, in_spec],
        out_specs=pl.BlockSpec((ce, D_MODEL, 2 * D_EXPERT), lambda i: (i, 0, 0)),
        compiler_params=pltpu.CompilerParams(dimension_semantics=("arbitrary",)),
        name="cast_gate_up",
    )(w_gate, w_up)


def _expert_ffn(g_ref, o_ref, q, wgu_ref, wd_ref):
    ch = g_ref.shape[0] // LANE_TILES
    xg = jnp.concatenate([g_ref[pl.ds(cc, ch, stride=LANE_TILES), :]
                          for cc in range(LANE_TILES)], axis=1).astype(BF16)
    h = jnp.dot(xg, wgu_ref[q], preferred_element_type=F32)
    he = (jax.nn.silu(h[:, :D_EXPERT]) * h[:, D_EXPERT:]).astype(BF16)
    o = jnp.dot(he, wd_ref[q], preferred_element_type=F32)
    for cc in range(LANE_TILES):
        o_ref[pl.ds(cc, ch, stride=LANE_TILES), :] = o[:, cc * LANES:(cc + 1) * LANES]


def _moe_kernel(dest_ref, offs_ref, z_ref, wcol_ref, wgu_ref, wd_ref, y_ref,
                inv_ref, g_ref, o_ref, gx_ref, ox_ref, y2_ref):
    i = pl.program_id(0)
    k = pl.program_id(1)
    eps = wd_ref.shape[0]
    e0 = k * eps
    tb = y_ref.shape[0]
    ns = 2 * tb
    ch = g_ref.shape[1] // LANE_TILES
    chx = gx_ref.shape[0] // LANE_TILES
    tok_mask = tb * LANE_TILES - 1
    rows_at = lambda off: pl.ds(pl.multiple_of(off, LANE_TILES), LANE_TILES)

    @pl.when((i == 0) & (k == 0))
    def _():
        o_ref[...] = jnp.zeros_like(o_ref)

    @pl.when(k == 0)
    def _():
        def pad_body(r, carry):
            inv_ref[ns + r] = ns * LANE_TILES
            return carry
        lax.fori_loop(0, ch, pad_body, 0)

        def inv_body(r, carry):
            for u in range(ROW_UNROLL):
                s = r * ROW_UNROLL + u
                inv_ref[dest_ref[0, 0, s]] = s * LANE_TILES
            return carry
        lax.fori_loop(0, ns // ROW_UNROLL, inv_body, 0)

        def gather_body(r, carry):
            for u in range(ROW_UNROLL):
                j = r * ROW_UNROLL + u
                g_ref[0, rows_at(j * LANE_TILES), :] = z_ref[rows_at(inv_ref[j] & tok_mask), :]
            return carry
        lax.fori_loop(0, ch // ROW_UNROLL, gather_body, 0)

    bounds = [offs_ref[0, 0, e0 + q] for q in range(eps + 1)]
    prev_start = jnp.where(k == 0, ns, offs_ref[0, 0, jnp.maximum(e0 - 1, 0)])

    for q in range(eps):
        cur, nxt = q % 2, 1 - q % 2
        scatter_base = prev_start if q == 0 else bounds[q - 1]
        for j in range(ch):
            y2_ref[rows_at(inv_ref[scatter_base + j]), :] = (
                o_ref[nxt, j * LANE_TILES:(j + 1) * LANE_TILES, :])
        _expert_ffn(g_ref.at[cur], o_ref.at[cur], q, wgu_ref, wd_ref)
        for j in range(ch):
            g_ref[nxt, j * LANE_TILES:(j + 1) * LANE_TILES, :] = (
                z_ref[rows_at(inv_ref[bounds[q + 1] + j] & tok_mask), :])

    for q in range(eps):
        start, end = bounds[q], bounds[q + 1]

        def extra_body(c, carry, start=start, q=q):
            a = start + ch + c * chx

            def gather_body(r, carry2):
                for u in range(SUBLANES):
                    j = r * SUBLANES + u
                    gx_ref[rows_at(j * LANE_TILES), :] = (
                        z_ref[rows_at(inv_ref[a + j] & tok_mask), :])
                return carry2
            lax.fori_loop(0, chx // SUBLANES, gather_body, 0)
            _expert_ffn(gx_ref, ox_ref, q, wgu_ref, wd_ref)

            def scatter_body(r, carry2):
                for u in range(SUBLANES):
                    j = r * SUBLANES + u
                    y2_ref[rows_at(inv_ref[a + j]), :] = ox_ref[rows_at(j * LANE_TILES), :]
                return carry2
            lax.fori_loop(0, chx // SUBLANES, scatter_body, 0)
            return carry
        lax.fori_loop(0, (jnp.maximum(end - start - ch, 0) + (chx - 1)) // chx, extra_body, 0)

    @pl.when(k == pl.num_programs(1) - 1)
    def _():
        last = (eps - 1) % 2

        def scatter_body(r, carry):
            for u in range(ROW_UNROLL):
                j = r * ROW_UNROLL + u
                y2_ref[rows_at(inv_ref[bounds[eps - 1] + j]), :] = (
                    o_ref[last, rows_at(j * LANE_TILES), :])
            return carry
        lax.fori_loop(0, ch // ROW_UNROLL, scatter_body, 0)
        w = wcol_ref[...]
        w1 = jnp.broadcast_to(w[:, 0:1], (tb, LANES))
        w2 = jnp.broadcast_to(w[:, 1:2], (tb, LANES))
        for cc in range(LANE_TILES):
            y_first = y2_ref[pl.ds(cc, tb, stride=LANE_TILES), :]
            y_second = y2_ref[pl.ds(tb * LANE_TILES + cc, tb, stride=LANE_TILES), :]
            y_ref[:, cc * LANES:(cc + 1) * LANES] = (
                w1 * y_first + w2 * y_second).astype(y_ref.dtype)


def _moe(x1lin, eidx, wcol, layer, w_gate_up, w_down):
    t = wcol.shape[0]
    tb, ch = TB_MOE, CH_MOE
    dest, offs = _plan(eidx, tb)
    eps = EXPERTS_PER_STEP
    assert eps % 2 == 0 and N_EXPERTS % eps == 0
    steps = N_EXPERTS // eps
    smem_spec = lambda n: pl.BlockSpec((1, 1, n), lambda i, e: (i, 0, 0), memory_space=pltpu.SMEM)
    expert_block = lambda i, k: (layer * steps + k, 0, 0)
    return pl.pallas_call(
        _moe_kernel,
        out_shape=jax.ShapeDtypeStruct((t, D_MODEL), BF16),
        grid=(t // tb, steps),
        in_specs=[smem_spec(2 * tb), smem_spec(LANES),
                  pl.BlockSpec((tb * LANE_TILES, LANES), lambda i, e: (i, 0),
                               pipeline_mode=pl.Buffered(1)),
                  pl.BlockSpec((tb, LANES), lambda i, e: (i, 0)),
                  pl.BlockSpec((eps, D_MODEL, 2 * D_EXPERT), expert_block),
                  pl.BlockSpec((eps, D_EXPERT, D_MODEL), expert_block)],
        out_specs=pl.BlockSpec((tb, D_MODEL), lambda i, e: (i, 0), pipeline_mode=pl.Buffered(1)),
        scratch_shapes=[pltpu.SMEM((2 * tb + ch,), jnp.int32),
                        pltpu.VMEM((2, ch * LANE_TILES, LANES), F32),
                        pltpu.VMEM((2, ch * LANE_TILES, LANES), F32),
                        pltpu.VMEM((CHX_MOE * LANE_TILES, LANES), F32),
                        pltpu.VMEM((CHX_MOE * LANE_TILES, LANES), F32),
                        pltpu.VMEM(((2 * tb + 1) * LANE_TILES, LANES), F32)],
        compiler_params=pltpu.CompilerParams(dimension_semantics=("arbitrary", "arbitrary"),
                                             vmem_limit_bytes=VMEM_LIMIT_BYTES),
        name="moe",
    )(dest, offs, x1lin, wcol, w_gate_up, w_down)


def _post_tile(x1lin_ref, y_ref, p_ref, lng_ref, lnb_ref, wp_ref, wg_ref, bg_ref):
    tm = y_ref.shape[0]
    x1 = jnp.concatenate([x1lin_ref[pl.ds(c, tm, stride=LANE_TILES), :]
                          for c in range(LANE_TILES)], axis=1)
    x2 = _layer_norm(ALPHA * x1 + y_ref[...].astype(F32), lng_ref[...], lnb_ref[...])
    emb = jnp.dot(p_ref[0].astype(BF16), wp_ref[0], preferred_element_type=F32)
    gate = jax.nn.sigmoid(
        jnp.dot(x2.astype(BF16), wg_ref[0], preferred_element_type=F32) + bg_ref[...])
    return x2 + gate * emb


def _load_tile(src):
    return src[0][...] if len(src) == 1 else _post_tile(*src)


def _plain_source(x):
    return (x,), lambda tm: [pl.BlockSpec((tm, D_MODEL), lambda i: (i, 0))]


def _post_source(x1lin, y, layer, p, ln_g, ln_b, w_proj, w_gate, b_gate):
    specs = lambda tm: [pl.BlockSpec((tm * LANE_TILES, LANES), lambda i: (i, 0)),
                        pl.BlockSpec((tm, D_MODEL), lambda i: (i, 0)),
                        pl.BlockSpec((1, tm, PLE_DIM), lambda i: (layer, i, 0)),
                        _const_spec(ln_g.shape), _const_spec(ln_b.shape),
                        _slot_spec(w_proj.shape, layer), _slot_spec(w_gate.shape, layer),
                        _const_spec(b_gate.shape)]
    return (x1lin, y, p, ln_g, ln_b, w_proj, w_gate, b_gate), specs


def _post_kernel(*refs):
    refs[-1][...] = _post_tile(*refs[:-1])


def _post(src, t):
    arrays, specs = src
    tm = TM_POST
    return pl.pallas_call(
        _post_kernel,
        out_shape=jax.ShapeDtypeStruct((t, D_MODEL), F32),
        grid=(t // tm,),
        in_specs=specs(tm),
        out_specs=pl.BlockSpec((tm, D_MODEL), lambda i: (i, 0)),
        compiler_params=pltpu.CompilerParams(dimension_semantics=("arbitrary",),
                                             vmem_limit_bytes=VMEM_LIMIT_BYTES),
        name="post_moe",
    )(*arrays)


def kernel(x, p, a_w_in, a_conv_w, a_w_out, b_w_in, b_w_out, c_w_in, c_ln_g, c_ln_b, c_w_s, c_b_s,
           c_w_out, router_w, router_b, moe_w_gate, moe_w_up, moe_w_down, ln_g, ln_b, ple_w_proj,
           ple_w_gate, ple_b_gate):
    bsz, seq, d = x.shape
    t = bsz * seq
    xt = x.reshape(t, d)
    pt = p.reshape(DEPTH, t, PLE_DIM)
    rwt = router_w.T.astype(BF16)
    rb = router_b.astype(F32).reshape(N_EXPERTS, 1)
    row = lambda a: a.reshape(1, -1)
    a_w_in_b, a_w_out_b = a_w_in.astype(BF16), a_w_out.astype(BF16)
    moe_shape = lambda w: w.astype(BF16).reshape((DEPTH * N_EXPERTS,) + w.shape[2:])
    moe_wgu, moe_wd = _cast_gate_up(moe_w_gate, moe_w_up), moe_shape(moe_w_down)
    ple_wp, ple_wg = ple_w_proj.astype(BF16), ple_w_gate.astype(BF16)
    src = _plain_source(xt)
    for i in range(DEPTH):
        mixer, slot = i % N_MIXERS, i // N_MIXERS
        g1, b1 = row(ln_g[i, 0]), row(ln_b[i, 0])
        if mixer == 0:
            outs = _conv_mixer(src, t, seq, slot, a_w_in_b, a_conv_w[slot], a_w_out_b, g1, b1, rwt,
                               rb)
        elif mixer == 1:
            outs = _retention_mixer(src, t, seq, slot, b_w_in.astype(BF16), b_w_out.astype(BF16),
                                    g1, b1, rwt, rb)
        else:
            outs = _sgu_mixer(src, t, slot, c_w_in.astype(BF16), row(c_ln_g[slot]),
                              row(c_ln_b[slot]), c_w_s[slot], c_b_s[slot][:, :, None],
                              c_w_out.astype(BF16), g1, b1, rwt, rb)
        x1lin, eidx, wcol = outs
        y = _moe(x1lin, eidx, wcol, i, moe_wgu, moe_wd)
        src = _post_source(x1lin, y, i, pt, row(ln_g[i, 1]), row(ln_b[i, 1]), ple_wp, ple_wg,
                           row(ple_b_gate[i]))
    return _post(src, t).reshape(bsz, seq, d)
```
